```python
import jax, jax.numpy as jnp
from jax import lax
import numpy as np

D_MODEL = 1024
BATCH = 16
SEQ = 4096
DEPTH = 2
DEC_BATCH = 32
DEC_SEQ = 64
PAST_LEN = 2048

CHUNK = 64
D_MIX = D_MODEL
ATTN_WIDTH = D_MIX // 2
CONV_CH = D_MIX - ATTN_WIDTH
HEAD_DIM = 64
N_HEADS = ATTN_WIDTH // HEAD_DIM
KV_HEADS = 2
GROUP = N_HEADS // KV_HEADS
WINDOW = 128
W_CHUNKS = WINDOW // CHUNK
CONV_WIDTH = 31
D_FF = 4 * D_MODEL
EPS = 1e-5
Q_DIM = N_HEADS * HEAD_DIM
KV_DIM = KV_HEADS * HEAD_DIM
IN_DIM = Q_DIM + 2 * KV_DIM + 2 * CONV_CH
SPLITS = (Q_DIM, Q_DIM + KV_DIM, Q_DIM + 2 * KV_DIM, Q_DIM + 2 * KV_DIM + CONV_CH)
NEG_INF = -1e30

kernel_name = "hymba_swa_sink_conformer_conv_stream_step"


def _rmsnorm(x, g):
    xf = x.astype(jnp.float32)
    y = xf * lax.rsqrt(jnp.mean(xf * xf, axis=-1, keepdims=True) + EPS)
    return (y * g.astype(jnp.float32)).astype(x.dtype)


def _layernorm(x, g, b):
    xf = x.astype(jnp.float32)
    mu = jnp.mean(xf, axis=-1, keepdims=True)
    xc = xf - mu
    var = jnp.mean(xc * xc, axis=-1, keepdims=True)
    y = xc * lax.rsqrt(var + EPS) * g.astype(jnp.float32) + b.astype(jnp.float32)
    return y.astype(x.dtype)


def _sink_softmax(s, sink):
    sk = sink.astype(jnp.float32)[:, :, None, None]
    m = jnp.maximum(jnp.max(s, axis=-1, keepdims=True), sk)
    p = jnp.exp(s - m)
    return p / (jnp.sum(p, axis=-1, keepdims=True) + jnp.exp(sk - m))


def _swa_prompt(q, k, v, sink):
    B, S = q.shape[:2]
    nc = S // CHUNK
    band = (W_CHUNKS + 1) * CHUNK
    qb = (q * (HEAD_DIM ** -0.5)).reshape(B, nc, CHUNK, KV_HEADS, GROUP, HEAD_DIM)
    pad = ((0, 0), (W_CHUNKS * CHUNK, 0), (0, 0), (0, 0))
    kc = jnp.pad(k, pad).reshape(B, nc + W_CHUNKS, CHUNK, KV_HEADS, HEAD_DIM)
    vc = jnp.pad(v, pad).reshape(B, nc + W_CHUNKS, CHUNK, KV_HEADS, HEAD_DIM)
    kb = jnp.concatenate([kc[:, j:j + nc] for j in range(W_CHUNKS + 1)], axis=2)
    vb = jnp.concatenate([vc[:, j:j + nc] for j in range(W_CHUNKS + 1)], axis=2)
    s = jnp.einsum('bcqkgd,bcjkd->bckgqj', qb, kb).astype(jnp.float32)
    key_chunk = jnp.arange(nc)[:, None] - W_CHUNKS + jnp.arange(band)[None, :] // CHUNK
    valid = key_chunk >= 0
    s = jnp.where(valid[None, :, None, None, None, :], s, NEG_INF)
    p = _sink_softmax(s, sink).astype(v.dtype)
    o = jnp.einsum('bckgqj,bcjkd->bcqkgd', p, vb).reshape(B, S, Q_DIM)
    keep = min(WINDOW, S)
    return o, k[:, S - keep:], v[:, S - keep:]


def _swa_sample(q, k, v, sink, cache_k, cache_v):
    B, T = q.shape[:2]
    kk = jnp.concatenate([cache_k, k], axis=1)
    vv = jnp.concatenate([cache_v, v], axis=1)
    qg = (q * (HEAD_DIM ** -0.5)).reshape(B, T, KV_HEADS, GROUP, HEAD_DIM)
    s = jnp.einsum('btkgd,bjkd->bkgtj', qg, kk).astype(jnp.float32)
    p = _sink_softmax(s, sink).astype(v.dtype)
    o = jnp.einsum('bkgtj,bjkd->btkgd', p, vv).reshape(B, T, Q_DIM)
    keep = cache_k.shape[1]
    return o, kk[:, -keep:], vv[:, -keep:]


def _conv_module(a, g, hist, conv_w, conv_b, ln_g, ln_b):
    u = a * jax.nn.sigmoid(g)
    full = jnp.concatenate([hist, u], axis=1)
    h = lax.conv_general_dilated(full, conv_w[:, None, :], window_strides=(1,), padding='VALID',
                                 dimension_numbers=('NWC', 'WIO', 'NWC'),
                                 feature_group_count=CONV_CH) + conv_b
    h = jax.nn.silu(_layernorm(h, ln_g, ln_b))
    return h, full[:, -(CONV_WIDTH - 1):]


def _layer(x, cache_k, cache_v, conv_hist, norm1, w_in, sink, conv_w, conv_b, ln_g, ln_b,
           w_out, norm2, w_up, w_down):
    B, T, _ = x.shape
    xn = _rmsnorm(x, norm1)
    proj = xn @ w_in
    q, k, v, a, g = jnp.split(proj, SPLITS, axis=-1)
    q = q.reshape(B, T, N_HEADS, HEAD_DIM)
    k = k.reshape(B, T, KV_HEADS, HEAD_DIM)
    v = v.reshape(B, T, KV_HEADS, HEAD_DIM)
    sink_g = sink.reshape(KV_HEADS, GROUP)
    if cache_k is None:
        o_attn, k_new, v_new = _swa_prompt(q, k, v, sink_g)
    else:
        o_attn, k_new, v_new = _swa_sample(q, k, v, sink_g, cache_k, cache_v)
    o_conv, conv_new = _conv_module(a, g, conv_hist, conv_w, conv_b, ln_g, ln_b)
    x = x + jnp.concatenate([o_attn, o_conv], axis=-1) @ w_out
    hn = _rmsnorm(x, norm2)
    x = x + jnp.square(jax.nn.relu(hn @ w_up)) @ w_down
    return x, k_new, v_new, conv_new


def setup_inputs(seed: int = 0) -> dict:
    key = jax.random.key(seed)
    ks = jax.random.split(key, 20)
    f32 = jnp.float32
    win = min(WINDOW, PAST_LEN)
    n = jax.random.normal
    return {
        "x_prompt": n(ks[0], (BATCH, SEQ, D_MODEL), f32),
        "x_sample": n(ks[1], (DEC_BATCH, DEC_SEQ, D_MODEL), f32),
        "cache_k": n(ks[2], (DEPTH, DEC_BATCH, win, KV_HEADS, HEAD_DIM), f32),
        "cache_v": n(ks[3], (DEPTH, DEC_BATCH, win, KV_HEADS, HEAD_DIM), f32),
        "state_conv": 0.5 * n(ks[4], (DEPTH, DEC_BATCH, CONV_WIDTH - 1, CONV_CH), f32),
        "norm1": 1.0 + 0.05 * n(ks[5], (DEPTH, D_MODEL), f32),
        "w_in": n(ks[6], (DEPTH, D_MODEL, IN_DIM), f32) * D_MODEL ** -0.5,
        "attn_sink": 0.5 * n(ks[7], (DEPTH, N_HEADS), f32),
        "conv_w": n(ks[8], (DEPTH, CONV_WIDTH, CONV_CH), f32) * CONV_WIDTH ** -0.5,
        "conv_b": 0.01 * n(ks[9], (DEPTH, CONV_CH), f32),
        "conv_ln_g": 1.0 + 0.05 * n(ks[10], (DEPTH, CONV_CH), f32),
        "conv_ln_b": 0.01 * n(ks[11], (DEPTH, CONV_CH), f32),
        "w_out": n(ks[12], (DEPTH, D_MIX, D_MODEL), f32) * D_MIX ** -0.5,
        "norm2": 1.0 + 0.05 * n(ks[13], (DEPTH, D_MODEL), f32),
        "w_up": n(ks[14], (DEPTH, D_MODEL, D_FF), f32) * D_MODEL ** -0.5,
        "w_down": n(ks[15], (DEPTH, D_FF, D_MODEL), f32) * D_FF ** -0.5,
        "final_norm": 1.0 + 0.05 * n(ks[16], (D_MODEL,), f32),
    }


def reference(x_prompt, x_sample, cache_k, cache_v, state_conv, norm1, w_in, attn_sink, conv_w,
              conv_b, conv_ln_g, conv_ln_b, w_out, norm2, w_up, w_down, final_norm):
    yp, ys = x_prompt, x_sample
    kp_l, vp_l, cp_l, ks_l, vs_l, cs_l = [], [], [], [], [], []
    for l in range(DEPTH):
        params = (norm1[l], w_in[l], attn_sink[l], conv_w[l], conv_b[l], conv_ln_g[l],
                  conv_ln_b[l], w_out[l], norm2[l], w_up[l], w_down[l])
        hist0 = jnp.zeros((yp.shape[0], CONV_WIDTH - 1, CONV_CH), yp.dtype)
        yp, kp, vp, cp = _layer(yp, None, None, hist0, *params)
        ys, k_s, v_s, c_s = _layer(ys, cache_k[l], cache_v[l], state_conv[l], *params)
        kp_l.append(kp); vp_l.append(vp); cp_l.append(cp)
        ks_l.append(k_s); vs_l.append(v_s); cs_l.append(c_s)
    y_prompt = _rmsnorm(yp, final_norm)
    y_sample = _rmsnorm(ys, final_norm)
    new_k_prompt = jnp.stack(kp_l)
    new_v_prompt = jnp.stack(vp_l)
    new_conv_prompt = jnp.stack(cp_l)
    new_k_sample = jnp.stack(ks_l)
    new_v_sample = jnp.stack(vs_l)
    new_conv_sample = jnp.stack(cs_l)
    return (y_prompt, y_sample, new_k_prompt, new_v_prompt, new_conv_prompt,
            new_k_sample, new_v_sample, new_conv_sample)
```

```python
import functools

import jax
import jax.numpy as jnp
from jax import lax
from jax.experimental import pallas as pl
from jax.experimental.pallas import tpu as pltpu

D_MODEL = 1024
CHUNK = 64
HEAD_DIM = 64
N_HEADS = 8
KV_HEADS = 2
GROUP = N_HEADS // KV_HEADS
WINDOW = 128
BAND = WINDOW + CHUNK
CONV_WIDTH = 31
CONV_CH = 512
D_FF = 4096
EPS = 1e-5
Q_DIM = N_HEADS * HEAD_DIM
KV_DIM = KV_HEADS * HEAD_DIM
IN_DIM = Q_DIM + 2 * KV_DIM + 2 * CONV_CH
A_OFF = Q_DIM + 2 * KV_DIM
G_OFF = A_OFF + CONV_CH
NEG_INF = -1e30

HIST = CONV_WIDTH - 1
HIST_PAD = 32
HIST_SKIP = HIST_PAD - HIST

PROMPT_TILE = 512
SAMPLE_SEGS = 8
FF_CHUNK = 1024
CONV_ROWS = 64
VMEM_LIMIT_BYTES = 56 * 1024 * 1024

BF16 = jnp.bfloat16
F32 = jnp.float32


def _rmsnorm(x, g):
    ms = jnp.mean(x * x, axis=-1, keepdims=True)
    return x * lax.rsqrt(ms + EPS) * g


def _dot(a, b):
    return jnp.dot(a, b, preferred_element_type=F32)


def _sink_column(sink_ref, kvh):
    return jnp.concatenate(
        [jnp.full((CHUNK, 1), sink_ref[kvh * GROUP + i], F32) for i in range(GROUP)], axis=0)


def _attend(q_rows, kband, vband, sink_col, limit):
    qs = jnp.concatenate(
        [q_rows[:, i * HEAD_DIM:(i + 1) * HEAD_DIM] for i in range(GROUP)], axis=0).astype(BF16)
    s = lax.dot_general(qs, kband, (((1,), (1,)), ((), ())), preferred_element_type=F32)
    if limit is not None:
        col = lax.broadcasted_iota(jnp.int32, s.shape, 1)
        s = jnp.where(col >= limit, s, NEG_INF)
    m = jnp.maximum(jnp.max(s, axis=-1, keepdims=True), sink_col)
    p = jnp.exp(s - m)
    denom = jnp.sum(p, axis=-1, keepdims=True) + jnp.exp(sink_col - m)
    o = _dot(p.astype(BF16), vband) / denom
    return jnp.concatenate([o[i * CHUNK:(i + 1) * CHUNK, :] for i in range(GROUP)], axis=1)


def _conv_block(u_window, cw_ref, cb_ref, lng_ref, lnb_ref, rows):
    acc = jnp.broadcast_to(cb_ref[...], (rows, CONV_CH))
    for j in range(CONV_WIDTH):
        acc = acc + u_window(j) * cw_ref[j:j + 1, :]
    mu = jnp.mean(acc, axis=-1, keepdims=True)
    xc = acc - mu
    var = jnp.mean(xc * xc, axis=-1, keepdims=True)
    y = xc * lax.rsqrt(var + EPS) * lng_ref[...] + lnb_ref[...]
    return y * jax.nn.sigmoid(y)


def _out_proj_and_mlp(x, ocat_ref, hn_ref, wout_ref, n2_ref, wup_ref, wdn_ref, fn_ref, o_ref, final):
    x1 = x + _dot(ocat_ref[...], wout_ref[...])
    o_ref[...] = x1
    hn_ref[...] = _rmsnorm(x1, n2_ref[...]).astype(BF16)
    for c in range(D_FF // FF_CHUNK):
        hid = _dot(hn_ref[...], wup_ref[:, c * FF_CHUNK:(c + 1) * FF_CHUNK])
        hid = jnp.square(jnp.maximum(hid, 0.0)).astype(BF16)
        o_ref[...] += _dot(hid, wdn_ref[c * FF_CHUNK:(c + 1) * FF_CHUNK, :])
    if final:
        o_ref[...] = _rmsnorm(o_ref[...], fn_ref[...])


def _prompt_kernel(sink_ref, x_ref, n1_ref, win_ref, cw_ref, cb_ref, lng_ref, lnb_ref, wout_ref,
                   n2_ref, wup_ref, wdn_ref, fn_ref,
                   o_ref, ok_ref, ov_ref, oc_ref,
                   xn_ref, q_ref, kbuf, vbuf, ubuf, ocat_ref, *, final):
    tile = PROMPT_TILE
    t = pl.program_id(1)

    @pl.when(t == 0)
    def _():
        kbuf[:, 0:WINDOW, :] = jnp.zeros((KV_HEADS, WINDOW, HEAD_DIM), BF16)
        vbuf[:, 0:WINDOW, :] = jnp.zeros((KV_HEADS, WINDOW, HEAD_DIM), BF16)
        ubuf[0:HIST_PAD, :] = jnp.zeros((HIST_PAD, CONV_CH), F32)

    x = x_ref[...]
    xn_ref[...] = _rmsnorm(x, n1_ref[...]).astype(BF16)

    q_ref[...] = _dot(xn_ref[...], win_ref[:, 0:Q_DIM])
    kv = _dot(xn_ref[...], win_ref[:, Q_DIM:A_OFF])
    ok_ref[...] = kv[tile - WINDOW:, 0:KV_DIM]
    ov_ref[...] = kv[tile - WINDOW:, KV_DIM:2 * KV_DIM]
    for j in range(KV_HEADS):
        kbuf[j, WINDOW:, :] = kv[:, j * HEAD_DIM:(j + 1) * HEAD_DIM].astype(BF16)
        vbuf[j, WINDOW:, :] = kv[:, KV_DIM + j * HEAD_DIM:KV_DIM + (j + 1) * HEAD_DIM].astype(BF16)
    a = _dot(xn_ref[...], win_ref[:, A_OFF:G_OFF])
    g = _dot(xn_ref[...], win_ref[:, G_OFF:IN_DIM])
    ubuf[HIST_PAD:, :] = a * jax.nn.sigmoid(g)
    oc_ref[...] = ubuf[tile:tile + HIST_PAD, :]

    for j in range(KV_HEADS):
        sink_col = _sink_column(sink_ref, j)
        for c in range(tile // CHUNK):
            limit = None
            if c < WINDOW // CHUNK:
                limit = jnp.where(t == 0, WINDOW - c * CHUNK, 0)
            o = _attend(q_ref[c * CHUNK:(c + 1) * CHUNK, j * GROUP * HEAD_DIM:(j + 1) * GROUP * HEAD_DIM],
                        kbuf[j, c * CHUNK:c * CHUNK + BAND, :], vbuf[j, c * CHUNK:c * CHUNK + BAND, :],
                        sink_col, limit)
            ocat_ref[c * CHUNK:(c + 1) * CHUNK, j * GROUP * HEAD_DIM:(j + 1) * GROUP * HEAD_DIM] = o.astype(BF16)

    for rb in range(tile // CONV_ROWS):
        base = rb * CONV_ROWS + HIST_SKIP
        o = _conv_block(lambda jj: ubuf[base + jj:base + jj + CONV_ROWS, :],
                        cw_ref, cb_ref, lng_ref, lnb_ref, CONV_ROWS)
        ocat_ref[rb * CONV_ROWS:(rb + 1) * CONV_ROWS, Q_DIM:] = o.astype(BF16)

    kbuf[:, 0:WINDOW, :] = kbuf[:, tile:tile + WINDOW, :]
    vbuf[:, 0:WINDOW, :] = vbuf[:, tile:tile + WINDOW, :]
    ubuf[0:HIST_PAD, :] = ubuf[tile:tile + HIST_PAD, :]

    _out_proj_and_mlp(x, ocat_ref, xn_ref, wout_ref, n2_ref, wup_ref, wdn_ref, fn_ref, o_ref, final)


def _sample_kernel(sink_ref, x_ref, ck_ref, cv_ref, hist_ref, n1_ref, win_ref, cw_ref, cb_ref, lng_ref,
                   lnb_ref, wout_ref, n2_ref, wup_ref, wdn_ref, fn_ref,
                   o_ref, ok_ref, ov_ref, oc_ref,
                   xn_ref, q_ref, kbuf, vbuf, ubuf, ocat_ref, *, final):
    segs = SAMPLE_SEGS
    x = x_ref[...]
    xn_ref[...] = _rmsnorm(x, n1_ref[...]).astype(BF16)

    q_ref[...] = _dot(xn_ref[...], win_ref[:, 0:Q_DIM])
    kv = _dot(xn_ref[...], win_ref[:, Q_DIM:A_OFF])
    a = _dot(xn_ref[...], win_ref[:, A_OFF:G_OFF])
    g = _dot(xn_ref[...], win_ref[:, G_OFF:IN_DIM])
    u = a * jax.nn.sigmoid(g)

    for b in range(segs):
        rows = slice(b * CHUNK, (b + 1) * CHUNK)
        ck = ck_ref[b]
        cv = cv_ref[b]
        kv_b = kv[rows, :]
        ok_ref[b, 0:WINDOW - CHUNK, :] = ck[CHUNK:, :]
        ok_ref[b, WINDOW - CHUNK:, :] = kv_b[:, 0:KV_DIM]
        ov_ref[b, 0:WINDOW - CHUNK, :] = cv[CHUNK:, :]
        ov_ref[b, WINDOW - CHUNK:, :] = kv_b[:, KV_DIM:]
        for j in range(KV_HEADS):
            kbuf[b, j, 0:WINDOW, :] = ck[:, j * HEAD_DIM:(j + 1) * HEAD_DIM].astype(BF16)
            kbuf[b, j, WINDOW:, :] = kv_b[:, j * HEAD_DIM:(j + 1) * HEAD_DIM].astype(BF16)
            vbuf[b, j, 0:WINDOW, :] = cv[:, j * HEAD_DIM:(j + 1) * HEAD_DIM].astype(BF16)
            vbuf[b, j, WINDOW:, :] = kv_b[:, KV_DIM + j * HEAD_DIM:KV_DIM + (j + 1) * HEAD_DIM].astype(BF16)
        ubuf[b, 0:HIST_PAD, :] = hist_ref[b]
        ubuf[b, HIST_PAD:, :] = u[rows, :]
        oc_ref[b] = ubuf[b, CHUNK:CHUNK + HIST_PAD, :]

    for j in range(KV_HEADS):
        sink_col = _sink_column(sink_ref, j)
        for b in range(segs):
            o = _attend(q_ref[b * CHUNK:(b + 1) * CHUNK, j * GROUP * HEAD_DIM:(j + 1) * GROUP * HEAD_DIM],
                        kbuf[b, j], vbuf[b, j], sink_col, None)
            ocat_ref[b * CHUNK:(b + 1) * CHUNK, j * GROUP * HEAD_DIM:(j + 1) * GROUP * HEAD_DIM] = o.astype(BF16)

    for b in range(segs):
        o = _conv_block(lambda jj: ubuf[b, HIST_SKIP + jj:HIST_SKIP + jj + CHUNK, :],
                        cw_ref, cb_ref, lng_ref, lnb_ref, CHUNK)
        ocat_ref[b * CHUNK:(b + 1) * CHUNK, Q_DIM:] = o.astype(BF16)

    _out_proj_and_mlp(x, ocat_ref, xn_ref, wout_ref, n2_ref, wup_ref, wdn_ref, fn_ref, o_ref, final)


def _const_spec(shape, grid_rank):
    zeros = (0,) * len(shape)
    if grid_rank == 1:
        return pl.BlockSpec(shape, lambda i: zeros, pipeline_mode=pl.Buffered(1))
    return pl.BlockSpec(shape, lambda i, k: zeros, pipeline_mode=pl.Buffered(1))


def _weight_specs(grid_rank):
    return [
        _const_spec((1, D_MODEL), grid_rank),
        _const_spec((D_MODEL, IN_DIM), grid_rank),
        _const_spec((CONV_WIDTH, CONV_CH), grid_rank),
        _const_spec((1, CONV_CH), grid_rank),
        _const_spec((1, CONV_CH), grid_rank),
        _const_spec((1, CONV_CH), grid_rank),
        _const_spec((D_MODEL, D_MODEL), grid_rank),
        _const_spec((1, D_MODEL), grid_rank),
        _const_spec((D_MODEL, D_FF), grid_rank),
        _const_spec((D_FF, D_MODEL), grid_rank),
        _const_spec((1, D_MODEL), grid_rank),
    ]


def _prompt_layer(x, sink, weights, final):
    batch, seq, _ = x.shape
    tile = PROMPT_TILE
    nt = seq // tile
    smem = pl.BlockSpec(memory_space=pltpu.SMEM)
    out_shape = (
        jax.ShapeDtypeStruct((batch, seq, D_MODEL), F32),
        jax.ShapeDtypeStruct((batch, WINDOW, KV_DIM), F32),
        jax.ShapeDtypeStruct((batch, WINDOW, KV_DIM), F32),
        jax.ShapeDtypeStruct((batch, HIST_PAD, CONV_CH), F32),
    )
    return pl.pallas_call(
        functools.partial(_prompt_kernel, final=final),
        grid=(batch, nt),
        in_specs=[smem, pl.BlockSpec((None, tile, D_MODEL), lambda b, t: (b, t, 0))] + _weight_specs(2),
        out_specs=(
            pl.BlockSpec((None, tile, D_MODEL), lambda b, t: (b, t, 0)),
            pl.BlockSpec((None, WINDOW, KV_DIM), lambda b, t: (b, 0, 0)),
            pl.BlockSpec((None, WINDOW, KV_DIM), lambda b, t: (b, 0, 0)),
            pl.BlockSpec((None, HIST_PAD, CONV_CH), lambda b, t: (b, 0, 0)),
        ),
        out_shape=out_shape,
        scratch_shapes=[
            pltpu.VMEM((tile, D_MODEL), BF16),
            pltpu.VMEM((tile, Q_DIM), F32),
            pltpu.VMEM((KV_HEADS, WINDOW + tile, HEAD_DIM), BF16),
            pltpu.VMEM((KV_HEADS, WINDOW + tile, HEAD_DIM), BF16),
            pltpu.VMEM((HIST_PAD + tile, CONV_CH), F32),
            pltpu.VMEM((tile, D_MODEL), BF16),
        ],
        compiler_params=pltpu.CompilerParams(
            dimension_semantics=("arbitrary", "arbitrary"), vmem_limit_bytes=VMEM_LIMIT_BYTES),
        name="prompt_layer",
    )(sink, x, *weights)


def _sample_layer(x2d, cache_k, cache_v, hist, sink, weights, final):
    rows_total = x2d.shape[0]
    segs = SAMPLE_SEGS
    rows = segs * CHUNK
    batch = rows_total // CHUNK
    smem = pl.BlockSpec(memory_space=pltpu.SMEM)
    out_shape = (
        jax.ShapeDtypeStruct((rows_total, D_MODEL), F32),
        jax.ShapeDtypeStruct((batch, WINDOW, KV_DIM), F32),
        jax.ShapeDtypeStruct((batch, WINDOW, KV_DIM), F32),
        jax.ShapeDtypeStruct((batch, HIST_PAD, CONV_CH), F32),
    )
    return pl.pallas_call(
        functools.partial(_sample_kernel, final=final),
        grid=(batch // segs,),
        in_specs=[
            smem,
            pl.BlockSpec((rows, D_MODEL), lambda i: (i, 0)),
            pl.BlockSpec((segs, WINDOW, KV_DIM), lambda i: (i, 0, 0)),
            pl.BlockSpec((segs, WINDOW, KV_DIM), lambda i: (i, 0, 0)),
            pl.BlockSpec((segs, HIST_PAD, CONV_CH), lambda i: (i, 0, 0)),
        ] + _weight_specs(1),
        out_specs=(
            pl.BlockSpec((rows, D_MODEL), lambda i: (i, 0)),
            pl.BlockSpec((segs, WINDOW, KV_DIM), lambda i: (i, 0, 0)),
            pl.BlockSpec((segs, WINDOW, KV_DIM), lambda i: (i, 0, 0)),
            pl.BlockSpec((segs, HIST_PAD, CONV_CH), lambda i: (i, 0, 0)),
        ),
        out_shape=out_shape,
        scratch_shapes=[
            pltpu.VMEM((rows, D_MODEL), BF16),
            pltpu.VMEM((rows, Q_DIM), F32),
            pltpu.VMEM((segs, KV_HEADS, BAND, HEAD_DIM), BF16),
            pltpu.VMEM((segs, KV_HEADS, BAND, HEAD_DIM), BF16),
            pltpu.VMEM((segs, HIST_PAD + CHUNK, CONV_CH), F32),
            pltpu.VMEM((rows, D_MODEL), BF16),
        ],
        compiler_params=pltpu.CompilerParams(
            dimension_semantics=("arbitrary",), vmem_limit_bytes=VMEM_LIMIT_BYTES),
        name="sample_layer",
    )(sink, x2d, cache_k, cache_v, hist, *weights)


def kernel(x_prompt, x_sample, cache_k, cache_v, state_conv, norm1, w_in, attn_sink, conv_w, conv_b,
           conv_ln_g, conv_ln_b, w_out, norm2, w_up, w_down, final_norm):
    depth = w_in.shape[0]
    dec_batch, dec_seq, _ = x_sample.shape
    batch = x_prompt.shape[0]

    col_scale = jnp.concatenate([jnp.full((Q_DIM,), HEAD_DIM ** -0.5, F32), jnp.ones((IN_DIM - Q_DIM,), F32)])
    w_in_b = (w_in * col_scale).astype(BF16)
    w_out_b = w_out.astype(BF16)
    w_up_b = w_up.astype(BF16)
    w_down_b = w_down.astype(BF16)
    hist = jnp.pad(state_conv, ((0, 0), (0, 0), (HIST_SKIP, 0), (0, 0)))
    ck = cache_k.reshape(depth, dec_batch, WINDOW, KV_DIM)
    cv = cache_v.reshape(depth, dec_batch, WINDOW, KV_DIM)

    yp = x_prompt
    ys = x_sample.reshape(dec_batch * dec_seq, D_MODEL)
    outs = [[] for _ in range(6)]
    for l in range(depth):
        final = l == depth - 1
        weights = (norm1[l][None], w_in_b[l], conv_w[l], conv_b[l][None], conv_ln_g[l][None],
                   conv_ln_b[l][None], w_out_b[l], norm2[l][None], w_up_b[l], w_down_b[l], final_norm[None])
        yp, kp, vp, cp = _prompt_layer(yp, attn_sink[l], weights, final)
        ys, k_s, v_s, c_s = _sample_layer(ys, ck[l], cv[l], hist[l], attn_sink[l], weights, final)
        for lst, val in zip(outs, (kp, vp, cp, k_s, v_s, c_s)):
            lst.append(val)

    kp, vp, cp, k_s, v_s, c_s = (jnp.stack(v) for v in outs)
    kv_shape_p = (depth, batch, WINDOW, KV_HEADS, HEAD_DIM)
    kv_shape_s = (depth, dec_batch, WINDOW, KV_HEADS, HEAD_DIM)
    return (yp, ys.reshape(dec_batch, dec_seq, D_MODEL),
            kp.reshape(kv_shape_p), vp.reshape(kv_shape_p), cp[:, :, HIST_SKIP:, :],
            k_s.reshape(kv_shape_s), v_s.reshape(kv_shape_s), c_s[:, :, HIST_SKIP:, :])
```

```python
import functools

import jax
import jax.numpy as jnp
from jax import lax
from jax.experimental import pallas as pl
from jax.experimental.pallas import tpu as pltpu

D_MODEL = 1024
CHUNK = 64
HEAD_DIM = 64
N_HEADS = 8
KV_HEADS = 2
GROUP = N_HEADS // KV_HEADS
WINDOW = 128
BAND = WINDOW + CHUNK
CONV_WIDTH = 31
CONV_CH = 512
D_FF = 4096
EPS = 1e-5
Q_DIM = N_HEADS * HEAD_DIM
KV_DIM = KV_HEADS * HEAD_DIM
IN_DIM = Q_DIM + 2 * KV_DIM + 2 * CONV_CH
A_OFF = Q_DIM + 2 * KV_DIM
G_OFF = A_OFF + CONV_CH
NEG_INF = -1e30

LANES = 128
SUBLANES = 8
HIST = CONV_WIDTH - 1
HIST_PAD = 32
HIST_SKIP = HIST_PAD - HIST

PROMPT_TILE = 512
SAMPLE_SEGS = 8
FF_CHUNK = 512
CONV_ROWS = 64
VMEM_LIMIT_BYTES = 56 * 1024 * 1024

BF16 = jnp.bfloat16
F32 = jnp.float32


def _rmsnorm(x, g):
    ms = jnp.mean(x * x, axis=-1, keepdims=True)
    return x * lax.rsqrt(ms + EPS) * g


_dot = functools.partial(jnp.dot, preferred_element_type=F32)


def _glu_tile(xn_ref, win_ref, lt):
    ag = _dot(xn_ref[...], win_ref[:, A_OFF + 2 * lt * LANES:A_OFF + 2 * (lt + 1) * LANES])
    return ag[:, 0:LANES] * jax.nn.sigmoid(ag[:, LANES:2 * LANES])


def _sink_column(sink_ref, kvh):
    return jnp.concatenate(
        [jnp.full((CHUNK, 1), sink_ref[kvh * GROUP + i], F32) for i in range(GROUP)], axis=0)


def _attend(q_rows, kband, vband, sink_col, limit):
    qs = jnp.concatenate(
        [q_rows[:, i * HEAD_DIM:(i + 1) * HEAD_DIM] for i in range(GROUP)], axis=0).astype(BF16)
    s = lax.dot_general(qs, kband, (((1,), (1,)), ((), ())), preferred_element_type=F32)
    if limit is not None:
        col = lax.broadcasted_iota(jnp.int32, s.shape, 1)
        s = jnp.where(col >= limit, s, NEG_INF)
    m = jnp.maximum(jnp.max(s, axis=-1, keepdims=True), sink_col)
    p = jnp.exp(s - m)
    denom = jnp.sum(p, axis=-1, keepdims=True) + jnp.exp(sink_col - m)
    o = _dot(p.astype(BF16), vband) / denom
    return jnp.concatenate([o[i * CHUNK:(i + 1) * CHUNK, :] for i in range(GROUP)], axis=1)


def _conv_block(window, cw_ref, cb_ref, lng_ref, lnb_ref, rows):
    n_out = rows // SUBLANES
    n_in = n_out + HIST_PAD // SUBLANES
    row_id = lax.broadcasted_iota(jnp.int32, (SUBLANES, LANES), 0)
    pieces = []
    for lt in range(CONV_CH // LANES):
        lanes = slice(lt * LANES, (lt + 1) * LANES)
        w = window(lt)
        accs = [jnp.broadcast_to(cb_ref[:, lanes], (SUBLANES, LANES))] * n_out
        rot_prev = None
        for grp in range(n_in):
            tile_in = w[grp * SUBLANES:(grp + 1) * SUBLANES, :]
            rot = [tile_in] + [pltpu.roll(tile_in, SUBLANES - r, axis=0) for r in range(1, SUBLANES)]
            if rot_prev is not None:
                for r in range(SUBLANES):
                    src = grp - 1
                    uses = [(src - a, a * SUBLANES + r - HIST_SKIP) for a in range(n_in - n_out + 1)]
                    uses = [(og, j) for og, j in uses if 0 <= og < n_out and 0 <= j < CONV_WIDTH]
                    if not uses:
                        continue
                    sl = rot_prev[0] if r == 0 else jnp.where(row_id < SUBLANES - r, rot_prev[r], rot[r])
                    for og, j in uses:
                        accs[og] = accs[og] + sl * cw_ref[j:j + 1, lanes]
            rot_prev = rot
        for og, j in [(n_in - 1 - a, a * SUBLANES - HIST_SKIP) for a in range(n_in - n_out + 1)]:
            if 0 <= og < n_out and 0 <= j < CONV_WIDTH:
                accs[og] = accs[og] + rot_prev[0] * cw_ref[j:j + 1, lanes]
        pieces.append(jnp.concatenate(accs, axis=0))
    acc = jnp.concatenate(pieces, axis=1)
    mu = jnp.mean(acc, axis=-1, keepdims=True)
    xc = acc - mu
    var = jnp.mean(xc * xc, axis=-1, keepdims=True)
    y = xc * lax.rsqrt(var + EPS) * lng_ref[...] + lnb_ref[...]
    return y * jax.nn.sigmoid(y)


def _out_proj(x, ocat_ref, wout_ref, n2_ref, x1_ref, hn_ref):
    x1 = x + _dot(ocat_ref[...], wout_ref[...])
    x1_ref[...] = x1
    hn_ref[...] = _rmsnorm(x1, n2_ref[...]).astype(BF16)


def _zero_after(value):
    rows, cols = value.shape
    tok = None
    for r in range(rows // SUBLANES):
        for c in range(cols // LANES):
            piece = value[r * SUBLANES:(r + 1) * SUBLANES, c * LANES:(c + 1) * LANES]
            tok = piece if tok is None else tok + piece
    bits = pltpu.bitcast(tok, jnp.uint32)
    bits = lax.shift_right_logical(lax.shift_right_logical(bits, jnp.uint32(16)), jnp.uint32(16))
    return pltpu.bitcast(bits, F32)


def _mlp_pass(c, x1_ref, hn_ref, wup_ref, wdn_ref, o_ref, after=None):
    hid = _dot(hn_ref[...], wup_ref[:, c * FF_CHUNK:(c + 1) * FF_CHUNK])
    hid = jnp.square(jnp.maximum(hid, 0.0))
    if after is not None:
        hid = hid + jnp.tile(after, (hid.shape[0] // SUBLANES, hid.shape[1] // LANES))
    hid = hid.astype(BF16)
    y = _dot(hid, wdn_ref[c * FF_CHUNK:(c + 1) * FF_CHUNK, :])
    if c == 0:
        o_ref[...] = x1_ref[...] + y
    else:
        o_ref[...] += y


def _mlp(x1_ref, hn_ref, wup_ref, wdn_ref, fn_ref, o_ref, final):
    for c in range(D_FF // FF_CHUNK):
        _mlp_pass(c, x1_ref, hn_ref, wup_ref, wdn_ref, o_ref)
    if final:
        o_ref[...] = _rmsnorm(o_ref[...], fn_ref[...])


def _prompt_kernel(sink_ref, x_ref, n1_ref, win_ref, cw_ref, cb_ref, lng_ref, lnb_ref, wout_ref,
                   n2_ref, wup_ref, wdn_ref, fn_ref,
                   o_ref, ok_ref, ov_ref, oc_ref,
                   xn_ref, q_ref, kbuf, vbuf, ubuf, ocat_ref, x1_ref, hn_ref, *, final, tiles_per_seq):
    tile = PROMPT_TILE
    g = pl.program_id(0)
    last = pl.num_programs(0) - 1
    t = jnp.minimum(g, last - 1) % tiles_per_seq
    slot = g % 2
    prev = 1 - slot

    @pl.when(g == 0)
    def _():
        x1_ref[prev] = jnp.zeros((tile, D_MODEL), F32)
        hn_ref[prev] = jnp.zeros((tile, D_MODEL), BF16)

    @pl.when(t == 0)
    def _():
        kbuf[:, 0:WINDOW, :] = jnp.zeros((KV_HEADS, WINDOW, HEAD_DIM), BF16)
        vbuf[:, 0:WINDOW, :] = jnp.zeros((KV_HEADS, WINDOW, HEAD_DIM), BF16)
        ubuf[0:HIST_PAD, :] = jnp.zeros((HIST_PAD, CONV_CH), F32)

    x = x_ref[...]
    xn_ref[...] = _rmsnorm(x, n1_ref[...]).astype(BF16)

    q_ref[...] = _dot(xn_ref[...], win_ref[:, 0:Q_DIM])
    kv = _dot(xn_ref[...], win_ref[:, Q_DIM:A_OFF])
    ok_ref[...] = kv[tile - WINDOW:, 0:KV_DIM]
    ov_ref[...] = kv[tile - WINDOW:, KV_DIM:2 * KV_DIM]
    for j in range(KV_HEADS):
        kbuf[j, WINDOW:, :] = kv[:, j * HEAD_DIM:(j + 1) * HEAD_DIM].astype(BF16)
        vbuf[j, WINDOW:, :] = kv[:, KV_DIM + j * HEAD_DIM:KV_DIM + (j + 1) * HEAD_DIM].astype(BF16)
    for lt in range(CONV_CH // LANES):
        ubuf[HIST_PAD:, lt * LANES:(lt + 1) * LANES] = _glu_tile(xn_ref, win_ref, lt)
    oc_ref[...] = ubuf[tile:tile + HIST_PAD, :]

    sink_cols = [_sink_column(sink_ref, j) for j in range(KV_HEADS)]

    def attention_unit(j, c):
        limit = None
        if c < WINDOW // CHUNK:
            limit = jnp.where(t == 0, WINDOW - c * CHUNK, 0)
        o = _attend(q_ref[c * CHUNK:(c + 1) * CHUNK, j * GROUP * HEAD_DIM:(j + 1) * GROUP * HEAD_DIM],
                    kbuf[j, c * CHUNK:c * CHUNK + BAND, :], vbuf[j, c * CHUNK:c * CHUNK + BAND, :],
                    sink_cols[j], limit)
        ocat_ref[c * CHUNK:(c + 1) * CHUNK, j * GROUP * HEAD_DIM:(j + 1) * GROUP * HEAD_DIM] = o.astype(BF16)
        return _zero_after(o)

    def conv_unit(rb):
        base = rb * CONV_ROWS
        o = _conv_block(lambda lt: ubuf[base:base + CONV_ROWS + HIST_PAD, lt * LANES:(lt + 1) * LANES],
                        cw_ref, cb_ref, lng_ref, lnb_ref, CONV_ROWS)
        ocat_ref[base:base + CONV_ROWS, Q_DIM:] = o.astype(BF16)
        return _zero_after(o)

    n_groups = tile // CHUNK
    n_pass = D_FF // FF_CHUNK
    for i in range(n_groups):
        after = conv_unit(i)
        for j in range(KV_HEADS):
            after = after + attention_unit(j, i)
        for c in range(i * n_pass // n_groups, (i + 1) * n_pass // n_groups):
            _mlp_pass(c, x1_ref.at[prev], hn_ref.at[prev], wup_ref, wdn_ref, o_ref, after)
    if final:
        o_ref[...] = _rmsnorm(o_ref[...], fn_ref[...])

    kbuf[:, 0:WINDOW, :] = kbuf[:, tile:tile + WINDOW, :]
    vbuf[:, 0:WINDOW, :] = vbuf[:, tile:tile + WINDOW, :]
    ubuf[0:HIST_PAD, :] = ubuf[tile:tile + HIST_PAD, :]

    _out_proj(x, ocat_ref, wout_ref, n2_ref, x1_ref.at[slot], hn_ref.at[slot])


def _sample_kernel(sink_ref, x_ref, ck_ref, cv_ref, hist_ref, n1_ref, win_ref, cw_ref, cb_ref, lng_ref,
                   lnb_ref, wout_ref, n2_ref, wup_ref, wdn_ref, fn_ref,
                   o_ref, ok_ref, ov_ref, oc_ref,
                   xn_ref, q_ref, kbuf, vbuf, ubuf, ocat_ref, x1_ref, *, final):
    segs = SAMPLE_SEGS
    x = x_ref[...]
    xn_ref[...] = _rmsnorm(x, n1_ref[...]).astype(BF16)

    q_ref[...] = _dot(xn_ref[...], win_ref[:, 0:Q_DIM])
    kv = _dot(xn_ref[...], win_ref[:, Q_DIM:A_OFF])
    for lt in range(CONV_CH // LANES):
        u = _glu_tile(xn_ref, win_ref, lt)
        for b in range(segs):
            ubuf[b, HIST_PAD:, lt * LANES:(lt + 1) * LANES] = u[b * CHUNK:(b + 1) * CHUNK, :]

    for b in range(segs):
        rows = slice(b * CHUNK, (b + 1) * CHUNK)
        ck = ck_ref[b]
        cv = cv_ref[b]
        kv_b = kv[rows, :]
        ok_ref[b, 0:WINDOW - CHUNK, :] = ck[CHUNK:, :]
        ok_ref[b, WINDOW - CHUNK:, :] = kv_b[:, 0:KV_DIM]
        ov_ref[b, 0:WINDOW - CHUNK, :] = cv[CHUNK:, :]
        ov_ref[b, WINDOW - CHUNK:, :] = kv_b[:, KV_DIM:]
        for j in range(KV_HEADS):
            kbuf[b, j, 0:WINDOW, :] = ck[:, j * HEAD_DIM:(j + 1) * HEAD_DIM].astype(BF16)
            kbuf[b, j, WINDOW:, :] = kv_b[:, j * HEAD_DIM:(j + 1) * HEAD_DIM].astype(BF16)
            vbuf[b, j, 0:WINDOW, :] = cv[:, j * HEAD_DIM:(j + 1) * HEAD_DIM].astype(BF16)
            vbuf[b, j, WINDOW:, :] = kv_b[:, KV_DIM + j * HEAD_DIM:KV_DIM + (j + 1) * HEAD_DIM].astype(BF16)
        ubuf[b, 0:HIST_PAD, :] = hist_ref[b]
        oc_ref[b] = ubuf[b, CHUNK:CHUNK + HIST_PAD, :]

    for j in range(KV_HEADS):
        sink_col = _sink_column(sink_ref, j)
        for b in range(segs):
            o = _attend(q_ref[b * CHUNK:(b + 1) * CHUNK, j * GROUP * HEAD_DIM:(j + 1) * GROUP * HEAD_DIM],
                        kbuf[b, j], vbuf[b, j], sink_col, None)
            ocat_ref[b * CHUNK:(b + 1) * CHUNK, j * GROUP * HEAD_DIM:(j + 1) * GROUP * HEAD_DIM] = o.astype(BF16)

    for b in range(segs):
        o = _conv_block(lambda lt: ubuf[b, :, lt * LANES:(lt + 1) * LANES],
                        cw_ref, cb_ref, lng_ref, lnb_ref, CHUNK)
        ocat_ref[b * CHUNK:(b + 1) * CHUNK, Q_DIM:] = o.astype(BF16)

    _out_proj(x, ocat_ref, wout_ref, n2_ref, x1_ref, xn_ref)
    _mlp(x1_ref, xn_ref, wup_ref, wdn_ref, fn_ref, o_ref, final)


def _const_spec(shape):
    zeros = (0,) * len(shape)
    return pl.BlockSpec(shape, lambda i: zeros, pipeline_mode=pl.Buffered(1))


def _weight_specs():
    return [
        _const_spec((1, D_MODEL)),
        _const_spec((D_MODEL, IN_DIM)),
        _const_spec((CONV_WIDTH, CONV_CH)),
        _const_spec((1, CONV_CH)),
        _const_spec((1, CONV_CH)),
        _const_spec((1, CONV_CH)),
        _const_spec((D_MODEL, D_MODEL)),
        _const_spec((1, D_MODEL)),
        _const_spec((D_MODEL, D_FF)),
        _const_spec((D_FF, D_MODEL)),
        _const_spec((1, D_MODEL)),
    ]


def _prompt_layer(x, sink, weights, final):
    batch, seq, _ = x.shape
    tile = PROMPT_TILE
    nt = seq // tile
    n_tiles = batch * nt
    smem = pl.BlockSpec(memory_space=pltpu.SMEM)

    def mixer_tile(g):
        return jnp.minimum(g, n_tiles - 1)

    def mlp_tile(g):
        return jnp.maximum(g - 1, 0)

    out_shape = (
        jax.ShapeDtypeStruct((batch, seq, D_MODEL), F32),
        jax.ShapeDtypeStruct((batch, WINDOW, KV_DIM), F32),
        jax.ShapeDtypeStruct((batch, WINDOW, KV_DIM), F32),
        jax.ShapeDtypeStruct((batch, HIST_PAD, CONV_CH), F32),
    )
    return pl.pallas_call(
        functools.partial(_prompt_kernel, final=final, tiles_per_seq=nt),
        grid=(n_tiles + 1,),
        in_specs=[smem, pl.BlockSpec((None, tile, D_MODEL),
                                     lambda g: (mixer_tile(g) // nt, mixer_tile(g) % nt, 0))] + _weight_specs(),
        out_specs=(
            pl.BlockSpec((None, tile, D_MODEL), lambda g: (mlp_tile(g) // nt, mlp_tile(g) % nt, 0)),
            pl.BlockSpec((None, WINDOW, KV_DIM), lambda g: (mixer_tile(g) // nt, 0, 0)),
            pl.BlockSpec((None, WINDOW, KV_DIM), lambda g: (mixer_tile(g) // nt, 0, 0)),
            pl.BlockSpec((None, HIST_PAD, CONV_CH), lambda g: (mixer_tile(g) // nt, 0, 0)),
        ),
        out_shape=out_shape,
        scratch_shapes=[
            pltpu.VMEM((tile, D_MODEL), BF16),
            pltpu.VMEM((tile, Q_DIM), F32),
            pltpu.VMEM((KV_HEADS, WINDOW + tile, HEAD_DIM), BF16),
            pltpu.VMEM((KV_HEADS, WINDOW + tile, HEAD_DIM), BF16),
            pltpu.VMEM((HIST_PAD + tile, CONV_CH), F32),
            pltpu.VMEM((tile, D_MODEL), BF16),
            pltpu.VMEM((2, tile, D_MODEL), F32),
            pltpu.VMEM((2, tile, D_MODEL), BF16),
        ],
        compiler_params=pltpu.CompilerParams(
            dimension_semantics=("arbitrary",), vmem_limit_bytes=VMEM_LIMIT_BYTES),
        name="prompt_layer",
    )(sink, x, *weights)


def _sample_layer(x2d, cache_k, cache_v, hist, sink, weights, final):
    rows_total = x2d.shape[0]
    segs = SAMPLE_SEGS
    rows = segs * CHUNK
    batch = rows_total // CHUNK
    smem = pl.BlockSpec(memory_space=pltpu.SMEM)
    out_shape = (
        jax.ShapeDtypeStruct((rows_total, D_MODEL), F32),
        jax.ShapeDtypeStruct((batch, WINDOW, KV_DIM), F32),
        jax.ShapeDtypeStruct((batch, WINDOW, KV_DIM), F32),
        jax.ShapeDtypeStruct((batch, HIST_PAD, CONV_CH), F32),
    )
    return pl.pallas_call(
        functools.partial(_sample_kernel, final=final),
        grid=(batch // segs,),
        in_specs=[
            smem,
            pl.BlockSpec((rows, D_MODEL), lambda i: (i, 0)),
            pl.BlockSpec((segs, WINDOW, KV_DIM), lambda i: (i, 0, 0)),
            pl.BlockSpec((segs, WINDOW, KV_DIM), lambda i: (i, 0, 0)),
            pl.BlockSpec((segs, HIST_PAD, CONV_CH), lambda i: (i, 0, 0)),
        ] + _weight_specs(),
        out_specs=(
            pl.BlockSpec((rows, D_MODEL), lambda i: (i, 0)),
            pl.BlockSpec((segs, WINDOW, KV_DIM), lambda i: (i, 0, 0)),
            pl.BlockSpec((segs, WINDOW, KV_DIM), lambda i: (i, 0, 0)),
            pl.BlockSpec((segs, HIST_PAD, CONV_CH), lambda i: (i, 0, 0)),
        ),
        out_shape=out_shape,
        scratch_shapes=[
            pltpu.VMEM((rows, D_MODEL), BF16),
            pltpu.VMEM((rows, Q_DIM), F32),
            pltpu.VMEM((segs, KV_HEADS, BAND, HEAD_DIM), BF16),
            pltpu.VMEM((segs, KV_HEADS, BAND, HEAD_DIM), BF16),
            pltpu.VMEM((segs, HIST_PAD + CHUNK, CONV_CH), F32),
            pltpu.VMEM((rows, D_MODEL), BF16),
            pltpu.VMEM((rows, D_MODEL), F32),
        ],
        compiler_params=pltpu.CompilerParams(
            dimension_semantics=("arbitrary",), vmem_limit_bytes=VMEM_LIMIT_BYTES),
        name="sample_layer",
    )(sink, x2d, cache_k, cache_v, hist, *weights)


def kernel(x_prompt, x_sample, cache_k, cache_v, state_conv, norm1, w_in, attn_sink, conv_w, conv_b,
           conv_ln_g, conv_ln_b, w_out, norm2, w_up, w_down, final_norm):
    depth = w_in.shape[0]
    dec_batch, dec_seq, _ = x_sample.shape
    batch = x_prompt.shape[0]

    col_scale = jnp.concatenate([jnp.full((Q_DIM,), HEAD_DIM ** -0.5, F32), jnp.ones((IN_DIM - Q_DIM,), F32)])
    glu_cols = (w_in[:, :, A_OFF:].reshape(depth, D_MODEL, 2, CONV_CH // LANES, LANES)
                .transpose(0, 1, 3, 2, 4).reshape(depth, D_MODEL, 2 * CONV_CH))
    w_in_b = (jnp.concatenate([w_in[:, :, :A_OFF], glu_cols], axis=-1) * col_scale).astype(BF16)
    w_out_b = w_out.astype(BF16)
    w_up_b = w_up.astype(BF16)
    w_down_b = w_down.astype(BF16)
    hist = jnp.pad(state_conv, ((0, 0), (0, 0), (HIST_SKIP, 0), (0, 0)))
    ck = cache_k.reshape(depth, dec_batch, WINDOW, KV_DIM)
    cv = cache_v.reshape(depth, dec_batch, WINDOW, KV_DIM)

    yp = x_prompt
    ys = x_sample.reshape(dec_batch * dec_seq, D_MODEL)
    outs = [[] for _ in range(6)]
    for l in range(depth):
        final = l == depth - 1
        weights = (norm1[l][None], w_in_b[l], conv_w[l], conv_b[l][None], conv_ln_g[l][None],
                   conv_ln_b[l][None], w_out_b[l], norm2[l][None], w_up_b[l], w_down_b[l], final_norm[None])
        yp, kp, vp, cp = _prompt_layer(yp, attn_sink[l], weights, final)
        ys, k_s, v_s, c_s = _sample_layer(ys, ck[l], cv[l], hist[l], attn_sink[l], weights, final)
        for lst, val in zip(outs, (kp, vp, cp, k_s, v_s, c_s)):
            lst.append(val)

    kp, vp, cp, k_s, v_s, c_s = (jnp.stack(v) for v in outs)
    kv_shape_p = (depth, batch, WINDOW, KV_HEADS, HEAD_DIM)
    kv_shape_s = (depth, dec_batch, WINDOW, KV_HEADS, HEAD_DIM)
    return (yp, ys.reshape(dec_batch, dec_seq, D_MODEL),
            kp.reshape(kv_shape_p), vp.reshape(kv_shape_p), cp[:, :, HIST_SKIP:, :],
            k_s.reshape(kv_shape_s), v_s.reshape(kv_shape_s), c_s[:, :, HIST_SKIP:, :])
```

```python
import functools

import jax
import jax.numpy as jnp
from jax import lax
from jax.experimental import pallas as pl
from jax.experimental.pallas import tpu as pltpu

D_MODEL = 1024
CHUNK = 64
HEAD_DIM = 64
N_HEADS = 8
KV_HEADS = 2
GROUP = N_HEADS // KV_HEADS
WINDOW = 128
BAND = WINDOW + CHUNK
CONV_WIDTH = 31
CONV_CH = 512
D_FF = 4096
EPS = 1e-5
Q_DIM = N_HEADS * HEAD_DIM
KV_DIM = KV_HEADS * HEAD_DIM
IN_DIM = Q_DIM + 2 * KV_DIM + 2 * CONV_CH
A_OFF = Q_DIM + 2 * KV_DIM
G_OFF = A_OFF + CONV_CH
NEG_INF = -1e30

LANES = 128
SUBLANES = 8
HIST = CONV_WIDTH - 1
HIST_PAD = 32
HIST_SKIP = HIST_PAD - HIST

PROMPT_TILE = 512
SAMPLE_SEGS = 8
FF_CHUNK = 512
CONV_ROWS = 64
VMEM_LIMIT_BYTES = 56 * 1024 * 1024
assert D_FF // FF_CHUNK == PROMPT_TILE // CHUNK == PROMPT_TILE // CONV_ROWS

BF16 = jnp.bfloat16
F32 = jnp.float32


def _rmsnorm(x, g):
    ms = jnp.mean(x * x, axis=-1, keepdims=True)
    return x * lax.rsqrt(ms + EPS) * g


_dot = functools.partial(jnp.dot, preferred_element_type=F32)


def _glu_tile(xn_ref, win_ref, lt):
    ag = _dot(xn_ref[...], win_ref[:, A_OFF + 2 * lt * LANES:A_OFF + 2 * (lt + 1) * LANES])
    return ag[:, 0:LANES] * jax.nn.sigmoid(ag[:, LANES:2 * LANES])


def _sink_column(sink_ref, kvh):
    return jnp.concatenate(
        [jnp.full((CHUNK, 1), sink_ref[kvh * GROUP + i], F32) for i in range(GROUP)], axis=0)


def _attn_probs(q_rows, kband, sink_col, limit):
    qs = jnp.concatenate(
        [q_rows[:, i * HEAD_DIM:(i + 1) * HEAD_DIM] for i in range(GROUP)], axis=0).astype(BF16)
    s = lax.dot_general(qs, kband, (((1,), (1,)), ((), ())), preferred_element_type=F32)
    if limit is not None:
        col = lax.broadcasted_iota(jnp.int32, s.shape, 1)
        s = jnp.where(col >= limit, s, NEG_INF)
    m = jnp.maximum(jnp.max(s, axis=-1, keepdims=True), sink_col)
    p = jnp.exp(s - m)
    denom = jnp.sum(p, axis=-1, keepdims=True) + jnp.exp(sink_col - m)
    return p.astype(BF16), denom


def _attn_values(p, denom, vband):
    o = _dot(p, vband) / denom
    return jnp.concatenate([o[i * CHUNK:(i + 1) * CHUNK, :] for i in range(GROUP)], axis=1)


def _conv_block(window, cw_ref, cb_ref, lng_ref, lnb_ref, rows):
    n_out = rows // SUBLANES
    n_in = n_out + HIST_PAD // SUBLANES
    row_id = lax.broadcasted_iota(jnp.int32, (SUBLANES, LANES), 0)
    pieces = []
    for lt in range(CONV_CH // LANES):
        lanes = slice(lt * LANES, (lt + 1) * LANES)
        w = window(lt)
        accs = [jnp.broadcast_to(cb_ref[:, lanes], (SUBLANES, LANES))] * n_out
        rot_prev = None
        for grp in range(n_in):
            tile_in = w[grp * SUBLANES:(grp + 1) * SUBLANES, :]
            rot = [tile_in] + [pltpu.roll(tile_in, SUBLANES - r, axis=0) for r in range(1, SUBLANES)]
            if rot_prev is not None:
                for r in range(SUBLANES):
                    src = grp - 1
                    uses = [(src - a, a * SUBLANES + r - HIST_SKIP) for a in range(n_in - n_out + 1)]
                    uses = [(og, j) for og, j in uses if 0 <= og < n_out and 0 <= j < CONV_WIDTH]
                    if not uses:
                        continue
                    sl = rot_prev[0] if r == 0 else jnp.where(row_id < SUBLANES - r, rot_prev[r], rot[r])
                    for og, j in uses:
                        accs[og] = accs[og] + sl * cw_ref[j:j + 1, lanes]
            rot_prev = rot
        for og, j in [(n_in - 1 - a, a * SUBLANES - HIST_SKIP) for a in range(n_in - n_out + 1)]:
            if 0 <= og < n_out and 0 <= j < CONV_WIDTH:
                accs[og] = accs[og] + rot_prev[0] * cw_ref[j:j + 1, lanes]
        pieces.append(jnp.concatenate(accs, axis=0))
    acc = jnp.concatenate(pieces, axis=1)
    mu = jnp.mean(acc, axis=-1, keepdims=True)
    xc = acc - mu
    var = jnp.mean(xc * xc, axis=-1, keepdims=True)
    y = xc * lax.rsqrt(var + EPS) * lng_ref[...] + lnb_ref[...]
    return y * jax.nn.sigmoid(y)


def _out_proj(x, ocat_ref, wout_ref, n2_ref, x1_ref, hn_ref):
    x1 = x + _dot(ocat_ref[...], wout_ref[...])
    x1_ref[...] = x1
    hn_ref[...] = _rmsnorm(x1, n2_ref[...]).astype(BF16)


def _zero_after(value):
    rows, cols = value.shape
    tok = None
    for r in range(rows // SUBLANES):
        for c in range(cols // LANES):
            piece = value[r * SUBLANES:(r + 1) * SUBLANES, c * LANES:(c + 1) * LANES]
            tok = piece if tok is None else tok + piece
    bits = pltpu.bitcast(tok, jnp.uint32)
    return lax.shift_right_logical(lax.shift_right_logical(bits, jnp.uint32(16)), jnp.uint32(16))


def _mlp_up(c, hn_ref, wup_ref):
    hid = _dot(hn_ref[...], wup_ref[:, c * FF_CHUNK:(c + 1) * FF_CHUNK])
    return jnp.square(jnp.maximum(hid, 0.0)).astype(BF16)


def _mlp_down(c, hid, x1_ref, wdn_ref, o_ref, after=None):
    if after is not None:
        zero = pltpu.bitcast(after, BF16)
        hid = hid + jnp.tile(zero, (hid.shape[0] // zero.shape[0], hid.shape[1] // zero.shape[1]))
    y = _dot(hid, wdn_ref[c * FF_CHUNK:(c + 1) * FF_CHUNK, :])
    if c == 0:
        o_ref[...] = x1_ref[...] + y
    else:
        o_ref[...] += y


def _mlp(x1_ref, hn_ref, wup_ref, wdn_ref, fn_ref, o_ref, final):
    for c in range(D_FF // FF_CHUNK):
        _mlp_down(c, _mlp_up(c, hn_ref, wup_ref), x1_ref, wdn_ref, o_ref)
    if final:
        o_ref[...] = _rmsnorm(o_ref[...], fn_ref[...])


def _prompt_kernel(sink_ref, x_ref, n1_ref, win_ref, cw_ref, cb_ref, lng_ref, lnb_ref, wout_ref,
                   n2_ref, wup_ref, wdn_ref, fn_ref,
                   o_ref, ok_ref, ov_ref, oc_ref,
                   xn_ref, q_ref, kbuf, vbuf, ubuf, ocat_ref, x1_ref, hn_ref, *, final, tiles_per_seq):
    tile = PROMPT_TILE
    g = pl.program_id(0)
    last = pl.num_programs(0) - 1
    t = jnp.minimum(g, last - 1) % tiles_per_seq
    slot = g % 2
    prev = 1 - slot

    @pl.when(g == 0)
    def _():
        x1_ref[prev] = jnp.zeros((tile, D_MODEL), F32)
        hn_ref[prev] = jnp.zeros((tile, D_MODEL), BF16)

    @pl.when(t == 0)
    def _():
        kbuf[:, 0:WINDOW, :] = jnp.zeros((KV_HEADS, WINDOW, HEAD_DIM), BF16)
        vbuf[:, 0:WINDOW, :] = jnp.zeros((KV_HEADS, WINDOW, HEAD_DIM), BF16)
        ubuf[0:HIST_PAD, :] = jnp.zeros((HIST_PAD, CONV_CH), F32)

    x1_prev = x1_ref.at[prev]
    hn_prev = hn_ref.at[prev]
    hid_next = _mlp_up(0, hn_prev, wup_ref)

    x = x_ref[...]
    xn_ref[...] = _rmsnorm(x, n1_ref[...]).astype(BF16)

    q_ref[...] = _dot(xn_ref[...], win_ref[:, 0:Q_DIM])
    kv = _dot(xn_ref[...], win_ref[:, Q_DIM:A_OFF])
    ok_ref[...] = kv[tile - WINDOW:, 0:KV_DIM]
    ov_ref[...] = kv[tile - WINDOW:, KV_DIM:2 * KV_DIM]
    for j in range(KV_HEADS):
        kbuf[j, WINDOW:, :] = kv[:, j * HEAD_DIM:(j + 1) * HEAD_DIM].astype(BF16)
        vbuf[j, WINDOW:, :] = kv[:, KV_DIM + j * HEAD_DIM:KV_DIM + (j + 1) * HEAD_DIM].astype(BF16)
    for lt in range(CONV_CH // LANES):
        ubuf[HIST_PAD:, lt * LANES:(lt + 1) * LANES] = _glu_tile(xn_ref, win_ref, lt)
    oc_ref[...] = ubuf[tile:tile + HIST_PAD, :]

    sink_cols = [_sink_column(sink_ref, j) for j in range(KV_HEADS)]

    def attn_slices(j, c):
        return (slice(c * CHUNK, (c + 1) * CHUNK), slice(j * GROUP * HEAD_DIM, (j + 1) * GROUP * HEAD_DIM),
                slice(c * CHUNK, c * CHUNK + BAND))

    def attn_probs_unit(j, c):
        rows, cols, band = attn_slices(j, c)
        limit = None
        if c < WINDOW // CHUNK:
            limit = jnp.where(t == 0, WINDOW - c * CHUNK, 0)
        return _attn_probs(q_ref[rows, cols], kbuf[j, band, :], sink_cols[j], limit)

    def attn_values_unit(j, c, probs):
        rows, cols, band = attn_slices(j, c)
        o = _attn_values(*probs, vbuf[j, band, :])
        ocat_ref[rows, cols] = o.astype(BF16)
        return _zero_after(o)

    def conv_unit(rb):
        base = rb * CONV_ROWS
        o = _conv_block(lambda lt: ubuf[base:base + CONV_ROWS + HIST_PAD, lt * LANES:(lt + 1) * LANES],
                        cw_ref, cb_ref, lng_ref, lnb_ref, CONV_ROWS)
        ocat_ref[base:base + CONV_ROWS, Q_DIM:] = o.astype(BF16)
        return _zero_after(o)

    n_groups = tile // CHUNK
    probs_next = [attn_probs_unit(j, 0) for j in range(KV_HEADS)]
    values_done = None
    for i in range(n_groups):
        hid, probs = hid_next, probs_next
        if i + 1 < n_groups:
            hid_next = _mlp_up(i + 1, hn_prev, wup_ref)
        values_now = None
        for j in range(KV_HEADS):
            zero = attn_values_unit(j, i, probs[j])
            values_now = zero if values_now is None else values_now | zero
        if i + 1 < n_groups:
            probs_next = [attn_probs_unit(j, i + 1) for j in range(KV_HEADS)]
        after = conv_unit(i)
        if i + 1 == n_groups:
            after = None
        for zero in (values_done,) if values_done is not None else ():
            after = zero if after is None else after | zero
        values_done = values_now
        _mlp_down(i, hid, x1_prev, wdn_ref, o_ref, after)
    if final:
        o_ref[...] = _rmsnorm(o_ref[...], fn_ref[...])

    kbuf[:, 0:WINDOW, :] = kbuf[:, tile:tile + WINDOW, :]
    vbuf[:, 0:WINDOW, :] = vbuf[:, tile:tile + WINDOW, :]
    ubuf[0:HIST_PAD, :] = ubuf[tile:tile + HIST_PAD, :]

    _out_proj(x, ocat_ref, wout_ref, n2_ref, x1_ref.at[slot], hn_ref.at[slot])


def _sample_kernel(sink_ref, x_ref, ck_ref, cv_ref, hist_ref, n1_ref, win_ref, cw_ref, cb_ref, lng_ref,
                   lnb_ref, wout_ref, n2_ref, wup_ref, wdn_ref, fn_ref,
                   o_ref, ok_ref, ov_ref, oc_ref,
                   xn_ref, q_ref, kbuf, vbuf, ubuf, ocat_ref, x1_ref, *, final):
    segs = SAMPLE_SEGS
    x = x_ref[...]
    xn_ref[...] = _rmsnorm(x, n1_ref[...]).astype(BF16)

    q_ref[...] = _dot(xn_ref[...], win_ref[:, 0:Q_DIM])
    kv = _dot(xn_ref[...], win_ref[:, Q_DIM:A_OFF])
    for lt in range(CONV_CH // LANES):
        u = _glu_tile(xn_ref, win_ref, lt)
        for b in range(segs):
            ubuf[b, HIST_PAD:, lt * LANES:(lt + 1) * LANES] = u[b * CHUNK:(b + 1) * CHUNK, :]

    for b in range(segs):
        rows = slice(b * CHUNK, (b + 1) * CHUNK)
        ck = ck_ref[b]
        cv = cv_ref[b]
        kv_b = kv[rows, :]
        ok_ref[b, 0:WINDOW - CHUNK, :] = ck[CHUNK:, :]
        ok_ref[b, WINDOW - CHUNK:, :] = kv_b[:, 0:KV_DIM]
        ov_ref[b, 0:WINDOW - CHUNK, :] = cv[CHUNK:, :]
        ov_ref[b, WINDOW - CHUNK:, :] = kv_b[:, KV_DIM:]
        for j in range(KV_HEADS):
            kbuf[b, j, 0:WINDOW, :] = ck[:, j * HEAD_DIM:(j + 1) * HEAD_DIM].astype(BF16)
            kbuf[b, j, WINDOW:, :] = kv_b[:, j * HEAD_DIM:(j + 1) * HEAD_DIM].astype(BF16)
            vbuf[b, j, 0:WINDOW, :] = cv[:, j * HEAD_DIM:(j + 1) * HEAD_DIM].astype(BF16)
            vbuf[b, j, WINDOW:, :] = kv_b[:, KV_DIM + j * HEAD_DIM:KV_DIM + (j + 1) * HEAD_DIM].astype(BF16)
        ubuf[b, 0:HIST_PAD, :] = hist_ref[b]
        oc_ref[b] = ubuf[b, CHUNK:CHUNK + HIST_PAD, :]

    units = [(j, b) for j in range(KV_HEADS) for b in range(segs)]
    sink_cols = [_sink_column(sink_ref, j) for j in range(KV_HEADS)]
    probs = [_attn_probs(q_ref[b * CHUNK:(b + 1) * CHUNK, j * GROUP * HEAD_DIM:(j + 1) * GROUP * HEAD_DIM],
                         kbuf[b, j], sink_cols[j], None) for j, b in units]

    for b in range(segs):
        o = _conv_block(lambda lt: ubuf[b, :, lt * LANES:(lt + 1) * LANES],
                        cw_ref, cb_ref, lng_ref, lnb_ref, CHUNK)
        ocat_ref[b * CHUNK:(b + 1) * CHUNK, Q_DIM:] = o.astype(BF16)

    for (j, b), unit_probs in zip(units, probs):
        o = _attn_values(*unit_probs, vbuf[b, j])
        ocat_ref[b * CHUNK:(b + 1) * CHUNK, j * GROUP * HEAD_DIM:(j + 1) * GROUP * HEAD_DIM] = o.astype(BF16)

    _out_proj(x, ocat_ref, wout_ref, n2_ref, x1_ref, xn_ref)
    _mlp(x1_ref, xn_ref, wup_ref, wdn_ref, fn_ref, o_ref, final)


def _const_spec(shape):
    zeros = (0,) * len(shape)
    return pl.BlockSpec(shape, lambda i: zeros, pipeline_mode=pl.Buffered(1))


def _weight_specs():
    return [
        _const_spec((1, D_MODEL)),
        _const_spec((D_MODEL, IN_DIM)),
        _const_spec((CONV_WIDTH, CONV_CH)),
        _const_spec((1, CONV_CH)),
        _const_spec((1, CONV_CH)),
        _const_spec((1, CONV_CH)),
        _const_spec((D_MODEL, D_MODEL)),
        _const_spec((1, D_MODEL)),
        _const_spec((D_MODEL, D_FF)),
        _const_spec((D_FF, D_MODEL)),
        _const_spec((1, D_MODEL)),
    ]


def _prompt_layer(x, sink, weights, final):
    batch, seq, _ = x.shape
    tile = PROMPT_TILE
    nt = seq // tile
    n_tiles = batch * nt
    smem = pl.BlockSpec(memory_space=pltpu.SMEM)

    def mixer_tile(g):
        return jnp.minimum(g, n_tiles - 1)

    def mlp_tile(g):
        return jnp.maximum(g - 1, 0)

    out_shape = (
        jax.ShapeDtypeStruct((batch, seq, D_MODEL), F32),
        jax.ShapeDtypeStruct((batch, WINDOW, KV_DIM), F32),
        jax.ShapeDtypeStruct((batch, WINDOW, KV_DIM), F32),
        jax.ShapeDtypeStruct((batch, HIST_PAD, CONV_CH), F32),
    )
    return pl.pallas_call(
        functools.partial(_prompt_kernel, final=final, tiles_per_seq=nt),
        grid=(n_tiles + 1,),
        in_specs=[smem, pl.BlockSpec((None, tile, D_MODEL),
                                     lambda g: (mixer_tile(g) // nt, mixer_tile(g) % nt, 0))] + _weight_specs(),
        out_specs=(
            pl.BlockSpec((None, tile, D_MODEL), lambda g: (mlp_tile(g) // nt, mlp_tile(g) % nt, 0)),
            pl.BlockSpec((None, WINDOW, KV_DIM), lambda g: (mixer_tile(g) // nt, 0, 0)),
            pl.BlockSpec((None, WINDOW, KV_DIM), lambda g: (mixer_tile(g) // nt, 0, 0)),
            pl.BlockSpec((None, HIST_PAD, CONV_CH), lambda g: (mixer_tile(g) // nt, 0, 0)),
        ),
        out_shape=out_shape,
        scratch_shapes=[
            pltpu.VMEM((tile, D_MODEL), BF16),
            pltpu.VMEM((tile, Q_DIM), F32),
            pltpu.VMEM((KV_HEADS, WINDOW + tile, HEAD_DIM), BF16),
            pltpu.VMEM((KV_HEADS, WINDOW + tile, HEAD_DIM), BF16),
            pltpu.VMEM((HIST_PAD + tile, CONV_CH), F32),
            pltpu.VMEM((tile, D_MODEL), BF16),
            pltpu.VMEM((2, tile, D_MODEL), F32),
            pltpu.VMEM((2, tile, D_MODEL), BF16),
        ],
        compiler_params=pltpu.CompilerParams(
            dimension_semantics=("arbitrary",), vmem_limit_bytes=VMEM_LIMIT_BYTES),
        name="prompt_layer",
    )(sink, x, *weights)


def _sample_layer(x2d, cache_k, cache_v, hist, sink, weights, final):
    rows_total = x2d.shape[0]
    segs = SAMPLE_SEGS
    rows = segs * CHUNK
    batch = rows_total // CHUNK
    smem = pl.BlockSpec(memory_space=pltpu.SMEM)
    out_shape = (
        jax.ShapeDtypeStruct((rows_total, D_MODEL), F32),
        jax.ShapeDtypeStruct((batch, WINDOW, KV_DIM), F32),
        jax.ShapeDtypeStruct((batch, WINDOW, KV_DIM), F32),
        jax.ShapeDtypeStruct((batch, HIST_PAD, CONV_CH), F32),
    )
    return pl.pallas_call(
        functools.partial(_sample_kernel, final=final),
        grid=(batch // segs,),
        in_specs=[
            smem,
            pl.BlockSpec((rows, D_MODEL), lambda i: (i, 0)),
            pl.BlockSpec((segs, WINDOW, KV_DIM), lambda i: (i, 0, 0)),
            pl.BlockSpec((segs, WINDOW, KV_DIM), lambda i: (i, 0, 0)),
            pl.BlockSpec((segs, HIST_PAD, CONV_CH), lambda i: (i, 0, 0)),
        ] + _weight_specs(),
        out_specs=(
            pl.BlockSpec((rows, D_MODEL), lambda i: (i, 0)),
            pl.BlockSpec((segs, WINDOW, KV_DIM), lambda i: (i, 0, 0)),
            pl.BlockSpec((segs, WINDOW, KV_DIM), lambda i: (i, 0, 0)),
            pl.BlockSpec((segs, HIST_PAD, CONV_CH), lambda i: (i, 0, 0)),
        ),
        out_shape=out_shape,
        scratch_shapes=[
            pltpu.VMEM((rows, D_MODEL), BF16),
            pltpu.VMEM((rows, Q_DIM), F32),
            pltpu.VMEM((segs, KV_HEADS, BAND, HEAD_DIM), BF16),
            pltpu.VMEM((segs, KV_HEADS, BAND, HEAD_DIM), BF16),
            pltpu.VMEM((segs, HIST_PAD + CHUNK, CONV_CH), F32),
            pltpu.VMEM((rows, D_MODEL), BF16),
            pltpu.VMEM((rows, D_MODEL), F32),
        ],
        compiler_params=pltpu.CompilerParams(
            dimension_semantics=("arbitrary",), vmem_limit_bytes=VMEM_LIMIT_BYTES),
        name="sample_layer",
    )(sink, x2d, cache_k, cache_v, hist, *weights)


def kernel(x_prompt, x_sample, cache_k, cache_v, state_conv, norm1, w_in, attn_sink, conv_w, conv_b,
           conv_ln_g, conv_ln_b, w_out, norm2, w_up, w_down, final_norm):
    depth = w_in.shape[0]
    dec_batch, dec_seq, _ = x_sample.shape
    batch = x_prompt.shape[0]

    col_scale = jnp.concatenate([jnp.full((Q_DIM,), HEAD_DIM ** -0.5, F32), jnp.ones((IN_DIM - Q_DIM,), F32)])
    glu_cols = (w_in[:, :, A_OFF:].reshape(depth, D_MODEL, 2, CONV_CH // LANES, LANES)
                .transpose(0, 1, 3, 2, 4).reshape(depth, D_MODEL, 2 * CONV_CH))
    w_in_b = (jnp.concatenate([w_in[:, :, :A_OFF], glu_cols], axis=-1) * col_scale).astype(BF16)
    w_out_b = w_out.astype(BF16)
    w_up_b = w_up.astype(BF16)
    w_down_b = w_down.astype(BF16)
    hist = jnp.pad(state_conv, ((0, 0), (0, 0), (HIST_SKIP, 0), (0, 0)))
    ck = cache_k.reshape(depth, dec_batch, WINDOW, KV_DIM)
    cv = cache_v.reshape(depth, dec_batch, WINDOW, KV_DIM)

    yp = x_prompt
    ys = x_sample.reshape(dec_batch * dec_seq, D_MODEL)
    outs = [[] for _ in range(6)]
    for l in range(depth):
        final = l == depth - 1
        weights = (norm1[l][None], w_in_b[l], conv_w[l], conv_b[l][None], conv_ln_g[l][None],
                   conv_ln_b[l][None], w_out_b[l], norm2[l][None], w_up_b[l], w_down_b[l], final_norm[None])
        yp, kp, vp, cp = _prompt_layer(yp, attn_sink[l], weights, final)
        ys, k_s, v_s, c_s = _sample_layer(ys, ck[l], cv[l], hist[l], attn_sink[l], weights, final)
        for lst, val in zip(outs, (kp, vp, cp, k_s, v_s, c_s)):
            lst.append(val)

    kp, vp, cp, k_s, v_s, c_s = (jnp.stack(v) for v in outs)
    kv_shape_p = (depth, batch, WINDOW, KV_HEADS, HEAD_DIM)
    kv_shape_s = (depth, dec_batch, WINDOW, KV_HEADS, HEAD_DIM)
    return (yp, ys.reshape(dec_batch, dec_seq, D_MODEL),
            kp.reshape(kv_shape_p), vp.reshape(kv_shape_p), cp[:, :, HIST_SKIP:, :],
            k_s.reshape(kv_shape_s), v_s.reshape(kv_shape_s), c_s[:, :, HIST_SKIP:, :])
```

```python
import functools

import jax
import jax.numpy as jnp
from jax import lax
from jax.experimental import pallas as pl
from jax.experimental.pallas import tpu as pltpu

D_MODEL = 1024
CHUNK = 64
HEAD_DIM = 64
N_HEADS = 8
KV_HEADS = 2
GROUP = N_HEADS // KV_HEADS
WINDOW = 128
BAND = WINDOW + CHUNK
CONV_WIDTH = 31
CONV_CH = 512
D_FF = 4096
EPS = 1e-5
Q_DIM = N_HEADS * HEAD_DIM
KV_DIM = KV_HEADS * HEAD_DIM
IN_DIM = Q_DIM + 2 * KV_DIM + 2 * CONV_CH
A_OFF = Q_DIM + 2 * KV_DIM
G_OFF = A_OFF + CONV_CH
NEG_INF = -1e30

LANES = 128
SUBLANES = 8
HIST = CONV_WIDTH - 1
HIST_PAD = 32
HIST_SKIP = HIST_PAD - HIST

PROMPT_TILE = 512
SAMPLE_SEGS = 8
FF_CHUNK = 512
CONV_ROWS = 64
SAMPLE_OUT_PROJ_BLOCKS = 2
VMEM_LIMIT_BYTES = 56 * 1024 * 1024
assert D_FF // FF_CHUNK == PROMPT_TILE // CHUNK == PROMPT_TILE // CONV_ROWS

BF16 = jnp.bfloat16
F32 = jnp.float32


def _rmsnorm(x, g):
    ms = jnp.mean(x * x, axis=-1, keepdims=True)
    return x * lax.rsqrt(ms + EPS) * g


_dot = functools.partial(jnp.dot, preferred_element_type=F32)


def _glu_tile(xn_ref, win_ref, lt):
    ag = _dot(xn_ref[...], win_ref[:, A_OFF + 2 * lt * LANES:A_OFF + 2 * (lt + 1) * LANES])
    return ag[:, 0:LANES] * jax.nn.sigmoid(ag[:, LANES:2 * LANES])


def _sink_column(sink_ref, kvh):
    return jnp.concatenate(
        [jnp.full((CHUNK, 1), sink_ref[kvh * GROUP + i], F32) for i in range(GROUP)], axis=0)


def _attn_probs(q_rows, kband, sink_col, limit):
    qs = jnp.concatenate(
        [q_rows[:, i * HEAD_DIM:(i + 1) * HEAD_DIM] for i in range(GROUP)], axis=0).astype(BF16)
    s = lax.dot_general(qs, kband, (((1,), (1,)), ((), ())), preferred_element_type=F32)
    if limit is not None:
        col = lax.broadcasted_iota(jnp.int32, s.shape, 1)
        s = jnp.where(col >= limit, s, NEG_INF)
    m = jnp.maximum(jnp.max(s, axis=-1, keepdims=True), sink_col)
    p = jnp.exp(s - m)
    denom = jnp.sum(p, axis=-1, keepdims=True) + jnp.exp(sink_col - m)
    return p.astype(BF16), denom


def _attn_values(p, denom, vband):
    o = _dot(p, vband) / denom
    return jnp.concatenate([o[i * CHUNK:(i + 1) * CHUNK, :] for i in range(GROUP)], axis=1)


def _conv_block(window, cw_ref, cb_ref, lng_ref, lnb_ref, rows):
    n_out = rows // SUBLANES
    n_in = n_out + HIST_PAD // SUBLANES
    row_id = lax.broadcasted_iota(jnp.int32, (SUBLANES, LANES), 0)
    pieces = []
    for lt in range(CONV_CH // LANES):
        lanes = slice(lt * LANES, (lt + 1) * LANES)
        w = window(lt)
        accs = [jnp.broadcast_to(cb_ref[:, lanes], (SUBLANES, LANES))] * n_out
        rot_prev = None
        for grp in range(n_in):
            tile_in = w[grp * SUBLANES:(grp + 1) * SUBLANES, :]
            rot = [tile_in] + [pltpu.roll(tile_in, SUBLANES - r, axis=0) for r in range(1, SUBLANES)]
            if rot_prev is not None:
                for r in range(SUBLANES):
                    src = grp - 1
                    uses = [(src - a, a * SUBLANES + r - HIST_SKIP) for a in range(n_in - n_out + 1)]
                    uses = [(og, j) for og, j in uses if 0 <= og < n_out and 0 <= j < CONV_WIDTH]
                    if not uses:
                        continue
                    sl = rot_prev[0] if r == 0 else jnp.where(row_id < SUBLANES - r, rot_prev[r], rot[r])
                    for og, j in uses:
                        accs[og] = accs[og] + sl * cw_ref[j:j + 1, lanes]
            rot_prev = rot
        for og, j in [(n_in - 1 - a, a * SUBLANES - HIST_SKIP) for a in range(n_in - n_out + 1)]:
            if 0 <= og < n_out and 0 <= j < CONV_WIDTH:
                accs[og] = accs[og] + rot_prev[0] * cw_ref[j:j + 1, lanes]
        pieces.append(jnp.concatenate(accs, axis=0))
    acc = jnp.concatenate(pieces, axis=1)
    mu = jnp.mean(acc, axis=-1, keepdims=True)
    xc = acc - mu
    var = jnp.mean(xc * xc, axis=-1, keepdims=True)
    y = xc * lax.rsqrt(var + EPS) * lng_ref[...] + lnb_ref[...]
    return y * jax.nn.sigmoid(y)


def _out_proj(x, ocat_ref, wout_ref, n2_ref, x1_ref, hn_ref, blocks=1):
    rows = x.shape[0] // blocks
    for r in range(blocks):
        blk = slice(r * rows, (r + 1) * rows)
        x1 = x[blk, :] + _dot(ocat_ref[blk, :], wout_ref[...])
        x1_ref[blk, :] = x1
        hn_ref[blk, :] = _rmsnorm(x1, n2_ref[...]).astype(BF16)


def _zero_after(value):
    rows, cols = value.shape
    tok = None
    for r in range(rows // SUBLANES):
        for c in range(cols // LANES):
            piece = value[r * SUBLANES:(r + 1) * SUBLANES, c * LANES:(c + 1) * LANES]
            tok = piece if tok is None else tok + piece
    bits = pltpu.bitcast(tok, jnp.uint32)
    return lax.shift_right_logical(lax.shift_right_logical(bits, jnp.uint32(16)), jnp.uint32(16))


def _mlp_up(c, hn_ref, wup_ref):
    hid = _dot(hn_ref[...], wup_ref[:, c * FF_CHUNK:(c + 1) * FF_CHUNK])
    return jnp.square(jnp.maximum(hid, 0.0)).astype(BF16)


def _mlp_down(c, hid, x1_ref, wdn_ref, o_ref, after=None):
    if after is not None:
        zero = pltpu.bitcast(after, BF16)
        hid = hid + jnp.tile(zero, (hid.shape[0] // zero.shape[0], hid.shape[1] // zero.shape[1]))
    y = _dot(hid, wdn_ref[c * FF_CHUNK:(c + 1) * FF_CHUNK, :])
    if c == 0:
        o_ref[...] = x1_ref[...] + y
    else:
        o_ref[...] += y


def _mlp(x1_ref, hn_ref, wup_ref, wdn_ref, fn_ref, o_ref, final):
    for c in range(D_FF // FF_CHUNK):
        _mlp_down(c, _mlp_up(c, hn_ref, wup_ref), x1_ref, wdn_ref, o_ref)
    if final:
        o_ref[...] = _rmsnorm(o_ref[...], fn_ref[...])


def _prompt_kernel(sink_ref, x_ref, n1_ref, win_ref, cw_ref, cb_ref, lng_ref, lnb_ref, wout_ref,
                   n2_ref, wup_ref, wdn_ref, fn_ref,
                   o_ref, ok_ref, ov_ref, oc_ref,
                   xn_ref, q_ref, kbuf, vbuf, ubuf, ocat_ref, x1_ref, hn_ref, *, final, tiles_per_seq):
    tile = PROMPT_TILE
    g = pl.program_id(0)
    last = pl.num_programs(0) - 1
    t = jnp.minimum(g, last - 1) % tiles_per_seq
    slot = g % 2
    prev = 1 - slot

    @pl.when(g == 0)
    def _():
        x1_ref[prev] = jnp.zeros((tile, D_MODEL), F32)
        hn_ref[prev] = jnp.zeros((tile, D_MODEL), BF16)

    @pl.when(t == 0)
    def _():
        kbuf[:, 0:WINDOW, :] = jnp.zeros((KV_HEADS, WINDOW, HEAD_DIM), BF16)
        vbuf[:, 0:WINDOW, :] = jnp.zeros((KV_HEADS, WINDOW, HEAD_DIM), BF16)
        ubuf[0:HIST_PAD, :] = jnp.zeros((HIST_PAD, CONV_CH), F32)

    x1_prev = x1_ref.at[prev]
    hn_prev = hn_ref.at[prev]
    hid_next = _mlp_up(0, hn_prev, wup_ref)

    x = x_ref[...]
    xn_ref[...] = _rmsnorm(x, n1_ref[...]).astype(BF16)

    q_ref[...] = _dot(xn_ref[...], win_ref[:, 0:Q_DIM])
    kv = _dot(xn_ref[...], win_ref[:, Q_DIM:A_OFF])
    ok_ref[...] = kv[tile - WINDOW:, 0:KV_DIM]
    ov_ref[...] = kv[tile - WINDOW:, KV_DIM:2 * KV_DIM]
    for j in range(KV_HEADS):
        kbuf[j, WINDOW:, :] = kv[:, j * HEAD_DIM:(j + 1) * HEAD_DIM].astype(BF16)
        vbuf[j, WINDOW:, :] = kv[:, KV_DIM + j * HEAD_DIM:KV_DIM + (j + 1) * HEAD_DIM].astype(BF16)
    for lt in range(CONV_CH // LANES):
        ubuf[HIST_PAD:, lt * LANES:(lt + 1) * LANES] = _glu_tile(xn_ref, win_ref, lt)
    oc_ref[...] = ubuf[tile:tile + HIST_PAD, :]

    sink_cols = [_sink_column(sink_ref, j) for j in range(KV_HEADS)]

    def attn_slices(j, c):
        return (slice(c * CHUNK, (c + 1) * CHUNK), slice(j * GROUP * HEAD_DIM, (j + 1) * GROUP * HEAD_DIM),
                slice(c * CHUNK, c * CHUNK + BAND))

    def attn_probs_unit(j, c):
        rows, cols, band = attn_slices(j, c)
        limit = None
        if c < WINDOW // CHUNK:
            limit = jnp.where(t == 0, WINDOW - c * CHUNK, 0)
        return _attn_probs(q_ref[rows, cols], kbuf[j, band, :], sink_cols[j], limit)

    def attn_values_unit(j, c, probs):
        rows, cols, band = attn_slices(j, c)
        o = _attn_values(*probs, vbuf[j, band, :])
        ocat_ref[rows, cols] = o.astype(BF16)
        return _zero_after(o)

    def conv_unit(rb):
        base = rb * CONV_ROWS
        o = _conv_block(lambda lt: ubuf[base:base + CONV_ROWS + HIST_PAD, lt * LANES:(lt + 1) * LANES],
                        cw_ref, cb_ref, lng_ref, lnb_ref, CONV_ROWS)
        ocat_ref[base:base + CONV_ROWS, Q_DIM:] = o.astype(BF16)
        return _zero_after(o)

    n_groups = tile // CHUNK
    probs_next = [attn_probs_unit(j, 0) for j in range(KV_HEADS)]
    values_done = None
    for i in range(n_groups):
        hid, probs = hid_next, probs_next
        if i + 1 < n_groups:
            hid_next = _mlp_up(i + 1, hn_prev, wup_ref)
        values_now = None
        for j in range(KV_HEADS):
            zero = attn_values_unit(j, i, probs[j])
            values_now = zero if values_now is None else values_now | zero
        if i + 1 < n_groups:
            probs_next = [attn_probs_unit(j, i + 1) for j in range(KV_HEADS)]
        after = conv_unit(i)
        if i + 1 == n_groups:
            after = None
        for zero in (values_done,) if values_done is not None else ():
            after = zero if after is None else after | zero
        values_done = values_now
        _mlp_down(i, hid, x1_prev, wdn_ref, o_ref, after)
    if final:
        o_ref[...] = _rmsnorm(o_ref[...], fn_ref[...])

    kbuf[:, 0:WINDOW, :] = kbuf[:, tile:tile + WINDOW, :]
    vbuf[:, 0:WINDOW, :] = vbuf[:, tile:tile + WINDOW, :]
    ubuf[0:HIST_PAD, :] = ubuf[tile:tile + HIST_PAD, :]

    _out_proj(x, ocat_ref, wout_ref, n2_ref, x1_ref.at[slot], hn_ref.at[slot])


def _sample_kernel(sink_ref, x_ref, ck_ref, cv_ref, hist_ref, n1_ref, win_ref, cw_ref, cb_ref, lng_ref,
                   lnb_ref, wout_ref, n2_ref, wup_ref, wdn_ref, fn_ref,
                   o_ref, ok_ref, ov_ref, oc_ref,
                   xn_ref, q_ref, kbuf, vbuf, ubuf, ocat_ref, x1_ref, *, final):
    segs = SAMPLE_SEGS
    x = x_ref[...]
    xn_ref[...] = _rmsnorm(x, n1_ref[...]).astype(BF16)

    q_ref[...] = _dot(xn_ref[...], win_ref[:, 0:Q_DIM])
    kv = _dot(xn_ref[...], win_ref[:, Q_DIM:A_OFF])
    for lt in range(CONV_CH // LANES):
        u = _glu_tile(xn_ref, win_ref, lt)
        for b in range(segs):
            ubuf[b, HIST_PAD:, lt * LANES:(lt + 1) * LANES] = u[b * CHUNK:(b + 1) * CHUNK, :]

    for b in range(segs):
        rows = slice(b * CHUNK, (b + 1) * CHUNK)
        ck = ck_ref[b]
        cv = cv_ref[b]
        kv_b = kv[rows, :]
        ok_ref[b, 0:WINDOW - CHUNK, :] = ck[CHUNK:, :]
        ok_ref[b, WINDOW - CHUNK:, :] = kv_b[:, 0:KV_DIM]
        ov_ref[b, 0:WINDOW - CHUNK, :] = cv[CHUNK:, :]
        ov_ref[b, WINDOW - CHUNK:, :] = kv_b[:, KV_DIM:]
        for j in range(KV_HEADS):
            kbuf[b, j, 0:WINDOW, :] = ck[:, j * HEAD_DIM:(j + 1) * HEAD_DIM].astype(BF16)
            kbuf[b, j, WINDOW:, :] = kv_b[:, j * HEAD_DIM:(j + 1) * HEAD_DIM].astype(BF16)
            vbuf[b, j, 0:WINDOW, :] = cv[:, j * HEAD_DIM:(j + 1) * HEAD_DIM].astype(BF16)
            vbuf[b, j, WINDOW:, :] = kv_b[:, KV_DIM + j * HEAD_DIM:KV_DIM + (j + 1) * HEAD_DIM].astype(BF16)
        ubuf[b, 0:HIST_PAD, :] = hist_ref[b]
        oc_ref[b] = ubuf[b, CHUNK:CHUNK + HIST_PAD, :]

    units = [(j, b) for j in range(KV_HEADS) for b in range(segs)]
    sink_cols = [_sink_column(sink_ref, j) for j in range(KV_HEADS)]
    probs = [_attn_probs(q_ref[b * CHUNK:(b + 1) * CHUNK, j * GROUP * HEAD_DIM:(j + 1) * GROUP * HEAD_DIM],
                         kbuf[b, j], sink_cols[j], None) for j, b in units]

    for b in range(segs):
        o = _conv_block(lambda lt: ubuf[b, :, lt * LANES:(lt + 1) * LANES],
                        cw_ref, cb_ref, lng_ref, lnb_ref, CHUNK)
        ocat_ref[b * CHUNK:(b + 1) * CHUNK, Q_DIM:] = o.astype(BF16)

    for (j, b), unit_probs in zip(units, probs):
        o = _attn_values(*unit_probs, vbuf[b, j])
        ocat_ref[b * CHUNK:(b + 1) * CHUNK, j * GROUP * HEAD_DIM:(j + 1) * GROUP * HEAD_DIM] = o.astype(BF16)

    _out_proj(x, ocat_ref, wout_ref, n2_ref, x1_ref, xn_ref, blocks=SAMPLE_OUT_PROJ_BLOCKS)
    _mlp(x1_ref, xn_ref, wup_ref, wdn_ref, fn_ref, o_ref, final)


def _layer_spec(shape, layer):
    index = (layer,) + (0,) * len(shape)
    return pl.BlockSpec((None,) + shape, lambda i: index, pipeline_mode=pl.Buffered(1))


def _weight_specs(layer):
    return [
        _layer_spec((1, D_MODEL), layer),
        _layer_spec((D_MODEL, IN_DIM), layer),
        _layer_spec((CONV_WIDTH, CONV_CH), layer),
        _layer_spec((1, CONV_CH), layer),
        _layer_spec((1, CONV_CH), layer),
        _layer_spec((1, CONV_CH), layer),
        _layer_spec((D_MODEL, D_MODEL), layer),
        _layer_spec((1, D_MODEL), layer),
        _layer_spec((D_MODEL, D_FF), layer),
        _layer_spec((D_FF, D_MODEL), layer),
        _layer_spec((1, D_MODEL), 0),
    ]


def _prompt_layer(x, sink, weights, layer, final):
    batch, seq, _ = x.shape
    tile = PROMPT_TILE
    nt = seq // tile
    n_tiles = batch * nt
    smem = pl.BlockSpec(memory_space=pltpu.SMEM)

    def mixer_tile(g):
        return jnp.minimum(g, n_tiles - 1)

    def mlp_tile(g):
        return jnp.maximum(g - 1, 0)

    out_shape = (
        jax.ShapeDtypeStruct((batch, seq, D_MODEL), F32),
        jax.ShapeDtypeStruct((batch, WINDOW, KV_DIM), F32),
        jax.ShapeDtypeStruct((batch, WINDOW, KV_DIM), F32),
        jax.ShapeDtypeStruct((batch, HIST_PAD, CONV_CH), F32),
    )
    return pl.pallas_call(
        functools.partial(_prompt_kernel, final=final, tiles_per_seq=nt),
        grid=(n_tiles + 1,),
        in_specs=[smem, pl.BlockSpec((None, tile, D_MODEL),
                                     lambda g: (mixer_tile(g) // nt, mixer_tile(g) % nt, 0))] + _weight_specs(layer),
        out_specs=(
            pl.BlockSpec((None, tile, D_MODEL), lambda g: (mlp_tile(g) // nt, mlp_tile(g) % nt, 0)),
            pl.BlockSpec((None, WINDOW, KV_DIM), lambda g: (mixer_tile(g) // nt, 0, 0)),
            pl.BlockSpec((None, WINDOW, KV_DIM), lambda g: (mixer_tile(g) // nt, 0, 0)),
            pl.BlockSpec((None, HIST_PAD, CONV_CH), lambda g: (mixer_tile(g) // nt, 0, 0)),
        ),
        out_shape=out_shape,
        scratch_shapes=[
            pltpu.VMEM((tile, D_MODEL), BF16),
            pltpu.VMEM((tile, Q_DIM), F32),
            pltpu.VMEM((KV_HEADS, WINDOW + tile, HEAD_DIM), BF16),
            pltpu.VMEM((KV_HEADS, WINDOW + tile, HEAD_DIM), BF16),
            pltpu.VMEM((HIST_PAD + tile, CONV_CH), F32),
            pltpu.VMEM((tile, D_MODEL), BF16),
            pltpu.VMEM((2, tile, D_MODEL), F32),
            pltpu.VMEM((2, tile, D_MODEL), BF16),
        ],
        compiler_params=pltpu.CompilerParams(
            dimension_semantics=("arbitrary",), vmem_limit_bytes=VMEM_LIMIT_BYTES),
        name="prompt_layer",
    )(sink, x, *weights)


def _sample_layer(x2d, cache_k, cache_v, hist, sink, weights, layer, final):
    rows_total = x2d.shape[0]
    segs = SAMPLE_SEGS
    rows = segs * CHUNK
    batch = rows_total // CHUNK
    smem = pl.BlockSpec(memory_space=pltpu.SMEM)
    out_shape = (
        jax.ShapeDtypeStruct((rows_total, D_MODEL), F32),
        jax.ShapeDtypeStruct((batch, WINDOW, KV_DIM), F32),
        jax.ShapeDtypeStruct((batch, WINDOW, KV_DIM), F32),
        jax.ShapeDtypeStruct((batch, HIST_PAD, CONV_CH), F32),
    )
    return pl.pallas_call(
        functools.partial(_sample_kernel, final=final),
        grid=(batch // segs,),
        in_specs=[
            smem,
            pl.BlockSpec((rows, D_MODEL), lambda i: (i, 0)),
            pl.BlockSpec((None, segs, WINDOW, KV_DIM), lambda i: (layer, i, 0, 0)),
            pl.BlockSpec((None, segs, WINDOW, KV_DIM), lambda i: (layer, i, 0, 0)),
            pl.BlockSpec((None, segs, HIST_PAD, CONV_CH), lambda i: (layer, i, 0, 0)),
        ] + _weight_specs(layer),
        out_specs=(
            pl.BlockSpec((rows, D_MODEL), lambda i: (i, 0)),
            pl.BlockSpec((segs, WINDOW, KV_DIM), lambda i: (i, 0, 0)),
            pl.BlockSpec((segs, WINDOW, KV_DIM), lambda i: (i, 0, 0)),
            pl.BlockSpec((segs, HIST_PAD, CONV_CH), lambda i: (i, 0, 0)),
        ),
        out_shape=out_shape,
        scratch_shapes=[
            pltpu.VMEM((rows, D_MODEL), BF16),
            pltpu.VMEM((rows, Q_DIM), F32),
            pltpu.VMEM((segs, KV_HEADS, BAND, HEAD_DIM), BF16),
            pltpu.VMEM((segs, KV_HEADS, BAND, HEAD_DIM), BF16),
            pltpu.VMEM((segs, HIST_PAD + CHUNK, CONV_CH), F32),
            pltpu.VMEM((rows, D_MODEL), BF16),
            pltpu.VMEM((rows, D_MODEL), F32),
        ],
        compiler_params=pltpu.CompilerParams(
            dimension_semantics=("arbitrary",), vmem_limit_bytes=VMEM_LIMIT_BYTES),
        name="sample_layer",
    )(sink, x2d, cache_k, cache_v, hist, *weights)


def kernel(x_prompt, x_sample, cache_k, cache_v, state_conv, norm1, w_in, attn_sink, conv_w, conv_b,
           conv_ln_g, conv_ln_b, w_out, norm2, w_up, w_down, final_norm):
    depth = w_in.shape[0]
    dec_batch, dec_seq, _ = x_sample.shape
    batch = x_prompt.shape[0]

    col_scale = jnp.concatenate([jnp.full((Q_DIM,), HEAD_DIM ** -0.5, F32), jnp.ones((IN_DIM - Q_DIM,), F32)])
    glu_cols = (w_in[:, :, A_OFF:].reshape(depth, D_MODEL, 2, CONV_CH // LANES, LANES)
                .transpose(0, 1, 3, 2, 4).reshape(depth, D_MODEL, 2 * CONV_CH))
    w_in_b = (jnp.concatenate([w_in[:, :, :A_OFF], glu_cols], axis=-1) * col_scale).astype(BF16)
    w_out_b = w_out.astype(BF16)
    w_up_b = w_up.astype(BF16)
    w_down_b = w_down.astype(BF16)
    hist = jnp.pad(state_conv, ((0, 0), (0, 0), (HIST_SKIP, 0), (0, 0)))
    ck = cache_k.reshape(depth, dec_batch, WINDOW, KV_DIM)
    cv = cache_v.reshape(depth, dec_batch, WINDOW, KV_DIM)

    weights = (norm1[:, None, :], w_in_b, conv_w, conv_b[:, None, :], conv_ln_g[:, None, :],
               conv_ln_b[:, None, :], w_out_b, norm2[:, None, :], w_up_b, w_down_b, final_norm[None, None, :])

    yp = x_prompt
    ys = x_sample.reshape(dec_batch * dec_seq, D_MODEL)
    outs = [[] for _ in range(6)]
    for l in range(depth):
        final = l == depth - 1
        yp, kp, vp, cp = _prompt_layer(yp, attn_sink[l], weights, l, final)
        ys, k_s, v_s, c_s = _sample_layer(ys, ck, cv, hist, attn_sink[l], weights, l, final)
        for lst, val in zip(outs, (kp, vp, cp, k_s, v_s, c_s)):
            lst.append(val)

    kp, vp, cp, k_s, v_s, c_s = (jnp.stack(v) for v in outs)
    kv_shape_p = (depth, batch, WINDOW, KV_HEADS, HEAD_DIM)
    kv_shape_s = (depth, dec_batch, WINDOW, KV_HEADS, HEAD_DIM)
    return (yp, ys.reshape(dec_batch, dec_seq, D_MODEL),
            kp.reshape(kv_shape_p), vp.reshape(kv_shape_p), cp[:, :, HIST_SKIP:, :],
            k_s.reshape(kv_shape_s), v_s.reshape(kv_shape_s), c_s[:, :, HIST_SKIP:, :])
```

```python
import functools

import jax
import jax.numpy as jnp
from jax import lax
from jax.experimental import pallas as pl
from jax.experimental.pallas import tpu as pltpu

D_MODEL = 1024
CHUNK = 64
HEAD_DIM = 64
N_HEADS = 8
KV_HEADS = 2
GROUP = N_HEADS // KV_HEADS
WINDOW = 128
BAND = WINDOW + CHUNK
CONV_WIDTH = 31
CONV_CH = 512
D_FF = 4096
EPS = 1e-5
Q_DIM = N_HEADS * HEAD_DIM
KV_DIM = KV_HEADS * HEAD_DIM
IN_DIM = Q_DIM + 2 * KV_DIM + 2 * CONV_CH
A_OFF = Q_DIM + 2 * KV_DIM
G_OFF = A_OFF + CONV_CH
NEG_INF = -1e30

LANES = 128
SUBLANES = 8
HIST = CONV_WIDTH - 1
HIST_PAD = 32
HIST_SKIP = HIST_PAD - HIST

PROMPT_TILE = 512
SAMPLE_SEGS = 8
FF_CHUNK = 512
CONV_ROWS = 64
SAMPLE_OUT_PROJ_BLOCKS = 2
VMEM_LIMIT_BYTES = 56 * 1024 * 1024
assert D_FF // FF_CHUNK == PROMPT_TILE // CHUNK == PROMPT_TILE // CONV_ROWS

BF16 = jnp.bfloat16
F32 = jnp.float32


def _rmsnorm(x, g):
    ms = jnp.mean(x * x, axis=-1, keepdims=True)
    return x * lax.rsqrt(ms + EPS) * g


_dot = functools.partial(jnp.dot, preferred_element_type=F32)


def _glu_tile(xn_ref, win_ref, lt):
    w_ag = jnp.concatenate([win_ref[:, A_OFF + lt * LANES:A_OFF + (lt + 1) * LANES],
                            win_ref[:, G_OFF + lt * LANES:G_OFF + (lt + 1) * LANES]], axis=1)
    ag = _dot(xn_ref[...], w_ag)
    return ag[:, 0:LANES] * jax.nn.sigmoid(ag[:, LANES:2 * LANES])


def _sink_column(sink_ref, kvh):
    return jnp.concatenate(
        [jnp.full((CHUNK, 1), sink_ref[kvh * GROUP + i], F32) for i in range(GROUP)], axis=0)


def _attn_probs(q_rows, kband, sink_col, limit):
    qs = jnp.concatenate(
        [q_rows[:, i * HEAD_DIM:(i + 1) * HEAD_DIM] for i in range(GROUP)], axis=0).astype(BF16)
    s = lax.dot_general(qs, kband, (((1,), (1,)), ((), ())), preferred_element_type=F32)
    if limit is not None:
        col = lax.broadcasted_iota(jnp.int32, s.shape, 1)
        s = jnp.where(col >= limit, s, NEG_INF)
    m = jnp.maximum(jnp.max(s, axis=-1, keepdims=True), sink_col)
    p = jnp.exp(s - m)
    denom = jnp.sum(p, axis=-1, keepdims=True) + jnp.exp(sink_col - m)
    return p.astype(BF16), denom


def _attn_values(p, denom, vband):
    o = _dot(p, vband) / denom
    return jnp.concatenate([o[i * CHUNK:(i + 1) * CHUNK, :] for i in range(GROUP)], axis=1)


def _conv_block(window, cw_ref, cb_ref, lng_ref, lnb_ref, rows):
    n_out = rows // SUBLANES
    n_in = n_out + HIST_PAD // SUBLANES
    row_id = lax.broadcasted_iota(jnp.int32, (SUBLANES, LANES), 0)
    pieces = []
    for lt in range(CONV_CH // LANES):
        lanes = slice(lt * LANES, (lt + 1) * LANES)
        w = window(lt)
        accs = [jnp.broadcast_to(cb_ref[:, lanes], (SUBLANES, LANES))] * n_out
        rot_prev = None
        for grp in range(n_in):
            tile_in = w[grp * SUBLANES:(grp + 1) * SUBLANES, :]
            rot = [tile_in] + [pltpu.roll(tile_in, SUBLANES - r, axis=0) for r in range(1, SUBLANES)]
            if rot_prev is not None:
                for r in range(SUBLANES):
                    src = grp - 1
                    uses = [(src - a, a * SUBLANES + r - HIST_SKIP) for a in range(n_in - n_out + 1)]
                    uses = [(og, j) for og, j in uses if 0 <= og < n_out and 0 <= j < CONV_WIDTH]
                    if not uses:
                        continue
                    sl = rot_prev[0] if r == 0 else jnp.where(row_id < SUBLANES - r, rot_prev[r], rot[r])
                    for og, j in uses:
                        accs[og] = accs[og] + sl * cw_ref[j:j + 1, lanes]
            rot_prev = rot
        for og, j in [(n_in - 1 - a, a * SUBLANES - HIST_SKIP) for a in range(n_in - n_out + 1)]:
            if 0 <= og < n_out and 0 <= j < CONV_WIDTH:
                accs[og] = accs[og] + rot_prev[0] * cw_ref[j:j + 1, lanes]
        pieces.append(jnp.concatenate(accs, axis=0))
    acc = jnp.concatenate(pieces, axis=1)
    mu = jnp.mean(acc, axis=-1, keepdims=True)
    xc = acc - mu
    var = jnp.mean(xc * xc, axis=-1, keepdims=True)
    y = xc * lax.rsqrt(var + EPS) * lng_ref[...] + lnb_ref[...]
    return y * jax.nn.sigmoid(y)


def _out_proj(x, ocat_ref, wout_ref, n2_ref, x1_ref, hn_ref, blocks=1):
    rows = x.shape[0] // blocks
    for r in range(blocks):
        blk = slice(r * rows, (r + 1) * rows)
        x1 = x[blk, :] + _dot(ocat_ref[blk, :], wout_ref[...])
        x1_ref[blk, :] = x1
        hn_ref[blk, :] = _rmsnorm(x1, n2_ref[...]).astype(BF16)


def _zero_after(value):
    rows, cols = value.shape
    tok = None
    for r in range(rows // SUBLANES):
        for c in range(cols // LANES):
            piece = value[r * SUBLANES:(r + 1) * SUBLANES, c * LANES:(c + 1) * LANES]
            tok = piece if tok is None else tok + piece
    bits = pltpu.bitcast(tok, jnp.uint32)
    return lax.shift_right_logical(lax.shift_right_logical(bits, jnp.uint32(16)), jnp.uint32(16))


def _mlp_up(c, hn_ref, wup_ref):
    hid = _dot(hn_ref[...], wup_ref[:, c * FF_CHUNK:(c + 1) * FF_CHUNK])
    return jnp.square(jnp.maximum(hid, 0.0)).astype(BF16)


def _mlp_down(c, hid, x1_ref, wdn_ref, o_ref, after=None):
    if after is not None:
        zero = pltpu.bitcast(after, BF16)
        hid = hid + jnp.tile(zero, (hid.shape[0] // zero.shape[0], hid.shape[1] // zero.shape[1]))
    y = _dot(hid, wdn_ref[c * FF_CHUNK:(c + 1) * FF_CHUNK, :])
    if c == 0:
        o_ref[...] = x1_ref[...] + y
    else:
        o_ref[...] += y


def _mlp(x1_ref, hn_ref, wup_ref, wdn_ref, fn_ref, o_ref, final):
    for c in range(D_FF // FF_CHUNK):
        _mlp_down(c, _mlp_up(c, hn_ref, wup_ref), x1_ref, wdn_ref, o_ref)
    if final:
        o_ref[...] = _rmsnorm(o_ref[...], fn_ref[...])


def _prompt_kernel(sink_ref, x_ref, n1_ref, win_ref, cw_ref, cb_ref, lng_ref, lnb_ref, wout_ref,
                   n2_ref, wup_ref, wdn_ref, fn_ref,
                   o_ref, ok_ref, ov_ref, oc_ref,
                   xn_ref, q_ref, kbuf, vbuf, ubuf, ocat_ref, x1_ref, hn_ref, *, final, tiles_per_seq):
    tile = PROMPT_TILE
    g = pl.program_id(0)
    last = pl.num_programs(0) - 1
    t = jnp.minimum(g, last - 1) % tiles_per_seq
    slot = g % 2
    prev = 1 - slot

    @pl.when(g == 0)
    def _():
        x1_ref[prev] = jnp.zeros((tile, D_MODEL), F32)
        hn_ref[prev] = jnp.zeros((tile, D_MODEL), BF16)

    @pl.when(t == 0)
    def _():
        kbuf[:, 0:WINDOW, :] = jnp.zeros((KV_HEADS, WINDOW, HEAD_DIM), BF16)
        vbuf[:, 0:WINDOW, :] = jnp.zeros((KV_HEADS, WINDOW, HEAD_DIM), BF16)
        ubuf[0:HIST_PAD, :] = jnp.zeros((HIST_PAD, CONV_CH), F32)

    x1_prev = x1_ref.at[prev]
    hn_prev = hn_ref.at[prev]
    hid_next = _mlp_up(0, hn_prev, wup_ref)

    x = x_ref[...]
    xn_ref[...] = _rmsnorm(x, n1_ref[...]).astype(BF16)

    q_ref[...] = _dot(xn_ref[...], win_ref[:, 0:Q_DIM])
    kv = _dot(xn_ref[...], win_ref[:, Q_DIM:A_OFF])
    ok_ref[...] = kv[tile - WINDOW:, 0:KV_DIM]
    ov_ref[...] = kv[tile - WINDOW:, KV_DIM:2 * KV_DIM]
    for j in range(KV_HEADS):
        kbuf[j, WINDOW:, :] = kv[:, j * HEAD_DIM:(j + 1) * HEAD_DIM].astype(BF16)
        vbuf[j, WINDOW:, :] = kv[:, KV_DIM + j * HEAD_DIM:KV_DIM + (j + 1) * HEAD_DIM].astype(BF16)
    for lt in range(CONV_CH // LANES):
        ubuf[HIST_PAD:, lt * LANES:(lt + 1) * LANES] = _glu_tile(xn_ref, win_ref, lt)
    oc_ref[...] = ubuf[tile:tile + HIST_PAD, :]

    sink_cols = [_sink_column(sink_ref, j) for j in range(KV_HEADS)]

    def attn_slices(j, c):
        return (slice(c * CHUNK, (c + 1) * CHUNK), slice(j * GROUP * HEAD_DIM, (j + 1) * GROUP * HEAD_DIM),
                slice(c * CHUNK, c * CHUNK + BAND))

    def attn_probs_unit(j, c):
        rows, cols, band = attn_slices(j, c)
        limit = None
        if c < WINDOW // CHUNK:
            limit = jnp.where(t == 0, WINDOW - c * CHUNK, 0)
        return _attn_probs(q_ref[rows, cols], kbuf[j, band, :], sink_cols[j], limit)

    def attn_values_unit(j, c, probs):
        rows, cols, band = attn_slices(j, c)
        o = _attn_values(*probs, vbuf[j, band, :])
        ocat_ref[rows, cols] = o.astype(BF16)
        return _zero_after(o)

    def conv_unit(rb):
        base = rb * CONV_ROWS
        o = _conv_block(lambda lt: ubuf[base:base + CONV_ROWS + HIST_PAD, lt * LANES:(lt + 1) * LANES],
                        cw_ref, cb_ref, lng_ref, lnb_ref, CONV_ROWS)
        ocat_ref[base:base + CONV_ROWS, Q_DIM:] = o.astype(BF16)
        return _zero_after(o)

    n_groups = tile // CHUNK
    probs_next = [attn_probs_unit(j, 0) for j in range(KV_HEADS)]
    values_done = None
    for i in range(n_groups):
        hid, probs = hid_next, probs_next
        if i + 1 < n_groups:
            hid_next = _mlp_up(i + 1, hn_prev, wup_ref)
        values_now = None
        for j in range(KV_HEADS):
            zero = attn_values_unit(j, i, probs[j])
            values_now = zero if values_now is None else values_now | zero
        if i + 1 < n_groups:
            probs_next = [attn_probs_unit(j, i + 1) for j in range(KV_HEADS)]
        after = conv_unit(i)
        if i + 1 == n_groups:
            after = None
        for zero in (values_done,) if values_done is not None else ():
            after = zero if after is None else after | zero
        values_done = values_now
        _mlp_down(i, hid, x1_prev, wdn_ref, o_ref, after)
    if final:
        o_ref[...] = _rmsnorm(o_ref[...], fn_ref[...])

    kbuf[:, 0:WINDOW, :] = kbuf[:, tile:tile + WINDOW, :]
    vbuf[:, 0:WINDOW, :] = vbuf[:, tile:tile + WINDOW, :]
    ubuf[0:HIST_PAD, :] = ubuf[tile:tile + HIST_PAD, :]

    _out_proj(x, ocat_ref, wout_ref, n2_ref, x1_ref.at[slot], hn_ref.at[slot])


def _sample_kernel(sink_ref, x_ref, ck_ref, cv_ref, hist_ref, n1_ref, win_ref, cw_ref, cb_ref, lng_ref,
                   lnb_ref, wout_ref, n2_ref, wup_ref, wdn_ref, fn_ref,
                   o_ref, ok_ref, ov_ref, oc_ref,
                   xn_ref, q_ref, kbuf, vbuf, ubuf, ocat_ref, x1_ref, *, final):
    segs = SAMPLE_SEGS
    x = x_ref[...]
    xn_ref[...] = _rmsnorm(x, n1_ref[...]).astype(BF16)

    q_ref[...] = _dot(xn_ref[...], win_ref[:, 0:Q_DIM])
    kv = _dot(xn_ref[...], win_ref[:, Q_DIM:A_OFF])
    for lt in range(CONV_CH // LANES):
        u = _glu_tile(xn_ref, win_ref, lt)
        for b in range(segs):
            ubuf[b, HIST_PAD:, lt * LANES:(lt + 1) * LANES] = u[b * CHUNK:(b + 1) * CHUNK, :]

    for b in range(segs):
        rows = slice(b * CHUNK, (b + 1) * CHUNK)
        ck = ck_ref[b]
        cv = cv_ref[b]
        kv_b = kv[rows, :]
        ok_ref[b, 0:WINDOW - CHUNK, :] = ck[CHUNK:, :]
        ok_ref[b, WINDOW - CHUNK:, :] = kv_b[:, 0:KV_DIM]
        ov_ref[b, 0:WINDOW - CHUNK, :] = cv[CHUNK:, :]
        ov_ref[b, WINDOW - CHUNK:, :] = kv_b[:, KV_DIM:]
        for j in range(KV_HEADS):
            kbuf[b, j, 0:WINDOW, :] = ck[:, j * HEAD_DIM:(j + 1) * HEAD_DIM].astype(BF16)
            kbuf[b, j, WINDOW:, :] = kv_b[:, j * HEAD_DIM:(j + 1) * HEAD_DIM].astype(BF16)
            vbuf[b, j, 0:WINDOW, :] = cv[:, j * HEAD_DIM:(j + 1) * HEAD_DIM].astype(BF16)
            vbuf[b, j, WINDOW:, :] = kv_b[:, KV_DIM + j * HEAD_DIM:KV_DIM + (j + 1) * HEAD_DIM].astype(BF16)
        ubuf[b, 0:HIST_PAD, :] = hist_ref[b]
        oc_ref[b] = ubuf[b, CHUNK:CHUNK + HIST_PAD, :]

    units = [(j, b) for j in range(KV_HEADS) for b in range(segs)]
    sink_cols = [_sink_column(sink_ref, j) for j in range(KV_HEADS)]
    probs = [_attn_probs(q_ref[b * CHUNK:(b + 1) * CHUNK, j * GROUP * HEAD_DIM:(j + 1) * GROUP * HEAD_DIM],
                         kbuf[b, j], sink_cols[j], None) for j, b in units]

    for b in range(segs):
        o = _conv_block(lambda lt: ubuf[b, :, lt * LANES:(lt + 1) * LANES],
                        cw_ref, cb_ref, lng_ref, lnb_ref, CHUNK)
        ocat_ref[b * CHUNK:(b + 1) * CHUNK, Q_DIM:] = o.astype(BF16)

    for (j, b), unit_probs in zip(units, probs):
        o = _attn_values(*unit_probs, vbuf[b, j])
        ocat_ref[b * CHUNK:(b + 1) * CHUNK, j * GROUP * HEAD_DIM:(j + 1) * GROUP * HEAD_DIM] = o.astype(BF16)

    _out_proj(x, ocat_ref, wout_ref, n2_ref, x1_ref, xn_ref, blocks=SAMPLE_OUT_PROJ_BLOCKS)
    _mlp(x1_ref, xn_ref, wup_ref, wdn_ref, fn_ref, o_ref, final)


def _layer_spec(shape, layer):
    index = (layer,) + (0,) * len(shape)
    return pl.BlockSpec((None,) + shape, lambda i: index, pipeline_mode=pl.Buffered(1))


def _weight_specs(layer):
    return [
        _layer_spec((1, D_MODEL), layer),
        _layer_spec((D_MODEL, IN_DIM), layer),
        _layer_spec((CONV_WIDTH, CONV_CH), layer),
        _layer_spec((1, CONV_CH), layer),
        _layer_spec((1, CONV_CH), layer),
        _layer_spec((1, CONV_CH), layer),
        _layer_spec((D_MODEL, D_MODEL), layer),
        _layer_spec((1, D_MODEL), layer),
        _layer_spec((D_MODEL, D_FF), layer),
        _layer_spec((D_FF, D_MODEL), layer),
        _layer_spec((1, D_MODEL), 0),
    ]


def _prompt_layer(x, sink, weights, layer, final):
    batch, seq, _ = x.shape
    tile = PROMPT_TILE
    nt = seq // tile
    n_tiles = batch * nt
    smem = pl.BlockSpec(memory_space=pltpu.SMEM)

    def mixer_tile(g):
        return jnp.minimum(g, n_tiles - 1)

    def mlp_tile(g):
        return jnp.maximum(g - 1, 0)

    out_shape = (
        jax.ShapeDtypeStruct((batch, seq, D_MODEL), F32),
        jax.ShapeDtypeStruct((batch, WINDOW, KV_DIM), F32),
        jax.ShapeDtypeStruct((batch, WINDOW, KV_DIM), F32),
        jax.ShapeDtypeStruct((batch, HIST_PAD, CONV_CH), F32),
    )
    return pl.pallas_call(
        functools.partial(_prompt_kernel, final=final, tiles_per_seq=nt),
        grid=(n_tiles + 1,),
        in_specs=[smem, pl.BlockSpec((None, tile, D_MODEL),
                                     lambda g: (mixer_tile(g) // nt, mixer_tile(g) % nt, 0))] + _weight_specs(layer),
        out_specs=(
            pl.BlockSpec((None, tile, D_MODEL), lambda g: (mlp_tile(g) // nt, mlp_tile(g) % nt, 0)),
            pl.BlockSpec((None, WINDOW, KV_DIM), lambda g: (mixer_tile(g) // nt, 0, 0)),
            pl.BlockSpec((None, WINDOW, KV_DIM), lambda g: (mixer_tile(g) // nt, 0, 0)),
            pl.BlockSpec((None, HIST_PAD, CONV_CH), lambda g: (mixer_tile(g) // nt, 0, 0)),
        ),
        out_shape=out_shape,
        scratch_shapes=[
            pltpu.VMEM((tile, D_MODEL), BF16),
            pltpu.VMEM((tile, Q_DIM), F32),
            pltpu.VMEM((KV_HEADS, WINDOW + tile, HEAD_DIM), BF16),
            pltpu.VMEM((KV_HEADS, WINDOW + tile, HEAD_DIM), BF16),
            pltpu.VMEM((HIST_PAD + tile, CONV_CH), F32),
            pltpu.VMEM((tile, D_MODEL), BF16),
            pltpu.VMEM((2, tile, D_MODEL), F32),
            pltpu.VMEM((2, tile, D_MODEL), BF16),
        ],
        compiler_params=pltpu.CompilerParams(
            dimension_semantics=("arbitrary",), vmem_limit_bytes=VMEM_LIMIT_BYTES),
        name="prompt_layer",
    )(sink, x, *weights)


def _sample_layer(x2d, cache_k, cache_v, hist, sink, weights, layer, final):
    rows_total = x2d.shape[0]
    segs = SAMPLE_SEGS
    rows = segs * CHUNK
    batch = rows_total // CHUNK
    smem = pl.BlockSpec(memory_space=pltpu.SMEM)
    out_shape = (
        jax.ShapeDtypeStruct((rows_total, D_MODEL), F32),
        jax.ShapeDtypeStruct((batch, WINDOW, KV_DIM), F32),
        jax.ShapeDtypeStruct((batch, WINDOW, KV_DIM), F32),
        jax.ShapeDtypeStruct((batch, HIST_PAD, CONV_CH), F32),
    )
    return pl.pallas_call(
        functools.partial(_sample_kernel, final=final),
        grid=(batch // segs,),
        in_specs=[
            smem,
            pl.BlockSpec((rows, D_MODEL), lambda i: (i, 0)),
            pl.BlockSpec((None, segs, WINDOW, KV_DIM), lambda i: (layer, i, 0, 0)),
            pl.BlockSpec((None, segs, WINDOW, KV_DIM), lambda i: (layer, i, 0, 0)),
            pl.BlockSpec((None, segs, HIST_PAD, CONV_CH), lambda i: (layer, i, 0, 0)),
        ] + _weight_specs(layer),
        out_specs=(
            pl.BlockSpec((rows, D_MODEL), lambda i: (i, 0)),
            pl.BlockSpec((segs, WINDOW, KV_DIM), lambda i: (i, 0, 0)),
            pl.BlockSpec((segs, WINDOW, KV_DIM), lambda i: (i, 0, 0)),
            pl.BlockSpec((segs, HIST_PAD, CONV_CH), lambda i: (i, 0, 0)),
        ),
        out_shape=out_shape,
        scratch_shapes=[
            pltpu.VMEM((rows, D_MODEL), BF16),
            pltpu.VMEM((rows, Q_DIM), F32),
            pltpu.VMEM((segs, KV_HEADS, BAND, HEAD_DIM), BF16),
            pltpu.VMEM((segs, KV_HEADS, BAND, HEAD_DIM), BF16),
            pltpu.VMEM((segs, HIST_PAD + CHUNK, CONV_CH), F32),
            pltpu.VMEM((rows, D_MODEL), BF16),
            pltpu.VMEM((rows, D_MODEL), F32),
        ],
        compiler_params=pltpu.CompilerParams(
            dimension_semantics=("arbitrary",), vmem_limit_bytes=VMEM_LIMIT_BYTES),
        name="sample_layer",
    )(sink, x2d, cache_k, cache_v, hist, *weights)


def kernel(x_prompt, x_sample, cache_k, cache_v, state_conv, norm1, w_in, attn_sink, conv_w, conv_b,
           conv_ln_g, conv_ln_b, w_out, norm2, w_up, w_down, final_norm):
    depth = w_in.shape[0]
    dec_batch, dec_seq, _ = x_sample.shape
    batch = x_prompt.shape[0]

    col_scale = jnp.concatenate([jnp.full((Q_DIM,), HEAD_DIM ** -0.5, F32), jnp.ones((IN_DIM - Q_DIM,), F32)])
    w_in_b = (w_in * col_scale).astype(BF16)
    w_out_b = w_out.astype(BF16)
    w_up_b = w_up.astype(BF16)
    w_down_b = w_down.astype(BF16)
    hist = jnp.pad(state_conv, ((0, 0), (0, 0), (HIST_SKIP, 0), (0, 0)))
    ck = cache_k.reshape(depth, dec_batch, WINDOW, KV_DIM)
    cv = cache_v.reshape(depth, dec_batch, WINDOW, KV_DIM)

    weights = (norm1[:, None, :], w_in_b, conv_w, conv_b[:, None, :], conv_ln_g[:, None, :],
               conv_ln_b[:, None, :], w_out_b, norm2[:, None, :], w_up_b, w_down_b, final_norm[None, None, :])

    yp = x_prompt
    ys = x_sample.reshape(dec_batch * dec_seq, D_MODEL)
    outs = [[] for _ in range(6)]
    for l in range(depth):
        final = l == depth - 1
        yp, kp, vp, cp = _prompt_layer(yp, attn_sink[l], weights, l, final)
        ys, k_s, v_s, c_s = _sample_layer(ys, ck, cv, hist, attn_sink[l], weights, l, final)
        for lst, val in zip(outs, (kp, vp, cp, k_s, v_s, c_s)):
            lst.append(val)

    kp, vp, cp, k_s, v_s, c_s = (jnp.stack(v) for v in outs)
    kv_shape_p = (depth, batch, WINDOW, KV_HEADS, HEAD_DIM)
    kv_shape_s = (depth, dec_batch, WINDOW, KV_HEADS, HEAD_DIM)
    return (yp, ys.reshape(dec_batch, dec_seq, D_MODEL),
            kp.reshape(kv_shape_p), vp.reshape(kv_shape_p), cp[:, :, HIST_SKIP:, :],
            k_s.reshape(kv_shape_s), v_s.reshape(kv_shape_s), c_s[:, :, HIST_SKIP:, :])
```
